```python
import math
import jax, jax.numpy as jnp
from jax import lax
import numpy as np

D_MODEL = 2048
BATCH = 8
SEQ = 2048
DEPTH = 1

HEAD_DIM = 64
MIX_WIDTH = D_MODEL
DIFF_WIDTH = MIX_WIDTH // 2
DIFF_HEADS = DIFF_WIDTH // (2 * HEAD_DIM)
SWA_WIDTH = MIX_WIDTH - DIFF_WIDTH
SWA_Q_HEADS = SWA_WIDTH // HEAD_DIM
SWA_KV_HEADS = SWA_Q_HEADS // 4
SWA_GROUP = SWA_Q_HEADS // SWA_KV_HEADS
WINDOW = 128
Q_BLOCK = 128
D_FF = -(-8 * D_MODEL // (3 * 256)) * 256
RMS_EPS = 1e-5

DIFF_Q_COLS = DIFF_HEADS * 2 * HEAD_DIM
DIFF_K_COLS = DIFF_HEADS * 2 * HEAD_DIM
DIFF_V_COLS = DIFF_HEADS * 2 * HEAD_DIM
SWA_Q_COLS = SWA_Q_HEADS * HEAD_DIM
SWA_K_COLS = SWA_KV_HEADS * HEAD_DIM
SWA_V_COLS = SWA_KV_HEADS * HEAD_DIM
IN_COLS = DIFF_Q_COLS + DIFF_K_COLS + DIFF_V_COLS + SWA_Q_COLS + SWA_K_COLS + SWA_V_COLS

kernel_name = "hymba_diffattn_swa_sink_alibi_swiglu"


def alibi_slopes(n_heads):
    return np.array([2.0 ** (-8.0 * (h + 1) / n_heads) for h in range(n_heads)], dtype=np.float32)


def lambda_init_fn(layer_idx):
    return 0.8 - 0.6 * math.exp(-0.3 * layer_idx)


def rmsnorm(x, w):
    xf = x.astype(jnp.float32)
    y = xf * lax.rsqrt(jnp.mean(xf * xf, axis=-1, keepdims=True) + RMS_EPS)
    return (y * w.astype(jnp.float32)).astype(x.dtype)


def diff_attention(q, k, v, lq1, lk1, lq2, lk2, subln_w, lambda_init):
    B, S = q.shape[0], q.shape[1]
    nblk = S // Q_BLOCK
    lam = (jnp.exp(jnp.sum(lq1.astype(jnp.float32) * lk1.astype(jnp.float32)))
           - jnp.exp(jnp.sum(lq2.astype(jnp.float32) * lk2.astype(jnp.float32)))
           + lambda_init)
    slopes = jnp.asarray(alibi_slopes(DIFF_HEADS))
    scale = HEAD_DIM ** -0.5
    kpos = jnp.arange(S)
    qb = q.reshape(B, nblk, Q_BLOCK, DIFF_HEADS, 2, HEAD_DIM).transpose(1, 0, 2, 3, 4, 5)

    def one_block(args):
        qblk, n = args
        qpos = n * Q_BLOCK + jnp.arange(Q_BLOCK)
        s = jnp.einsum('bqhcd,bkhcd->bhcqk', qblk, k).astype(jnp.float32) * scale
        dist = (qpos[:, None] - kpos[None, :]).astype(jnp.float32)
        s = s - slopes[None, :, None, None, None] * dist
        s = jnp.where(dist >= 0, s, -jnp.inf)
        p = jax.nn.softmax(s, axis=-1)
        a = p[:, :, 0] - lam * p[:, :, 1]
        return jnp.einsum('bhqk,bkhe->bqhe', a.astype(v.dtype), v)

    o = lax.map(one_block, (qb, jnp.arange(nblk)))
    o = o.transpose(1, 0, 2, 3, 4).reshape(B, S, DIFF_HEADS, 2 * HEAD_DIM)
    o = rmsnorm(o, subln_w) * (1.0 - lambda_init)
    return o.reshape(B, S, DIFF_HEADS * 2 * HEAD_DIM)


def swa_sink_attention(q, k, v, sinks):
    B, S = q.shape[0], q.shape[1]
    nblk = S // WINDOW
    scale = HEAD_DIM ** -0.5
    qb = q.reshape(B, nblk, WINDOW, SWA_KV_HEADS, SWA_GROUP, HEAD_DIM)

    def band(t):
        tp = jnp.pad(t, ((0, 0), (WINDOW, 0), (0, 0), (0, 0)))
        tb = tp.reshape(B, nblk + 1, WINDOW, SWA_KV_HEADS, HEAD_DIM)
        return jnp.concatenate([tb[:, :-1], tb[:, 1:]], axis=2)

    kb, vb = band(k), band(v)
    s = jnp.einsum('bnqhgd,bnkhd->bnhgqk', qb, kb).astype(jnp.float32) * scale
    i = jnp.arange(WINDOW)[:, None]
    j = jnp.arange(2 * WINDOW)[None, :]
    dist = WINDOW + i - j
    blk = jnp.arange(nblk)[:, None, None]
    valid = (dist >= 0) & (dist < WINDOW) & (blk * WINDOW - WINDOW + j >= 0)
    slopes = jnp.asarray(alibi_slopes(SWA_Q_HEADS)).reshape(SWA_KV_HEADS, SWA_GROUP)
    s = s - slopes[:, :, None, None] * dist.astype(jnp.float32)
    s = jnp.where(valid[None, :, None, None], s, -jnp.inf)
    sink = jnp.broadcast_to(
        sinks.astype(jnp.float32).reshape(SWA_KV_HEADS, SWA_GROUP)[None, None, :, :, None, None],
        s.shape[:-1] + (1,))
    p = jax.nn.softmax(jnp.concatenate([s, sink], axis=-1), axis=-1)[..., :-1]
    o = jnp.einsum('bnhgqk,bnkhd->bnqhgd', p.astype(v.dtype), vb)
    return o.reshape(B, S, SWA_Q_HEADS * HEAD_DIM)


def setup_inputs(seed: int = 0) -> dict:
    key = jax.random.key(seed)
    ks = jax.random.split(key, 16)
    f32 = jnp.float32
    nrm = lambda k, shape, s: jax.random.normal(k, shape, f32) * s
    return {
        "x": nrm(ks[0], (BATCH, SEQ, D_MODEL), 1.0),
        "attn_norm_w": 1.0 + nrm(ks[1], (DEPTH, D_MODEL), 0.02),
        "w_in": nrm(ks[2], (DEPTH, D_MODEL, IN_COLS), D_MODEL ** -0.5),
        "lambda_q1": nrm(ks[3], (DEPTH, HEAD_DIM), 0.1),
        "lambda_k1": nrm(ks[4], (DEPTH, HEAD_DIM), 0.1),
        "lambda_q2": nrm(ks[5], (DEPTH, HEAD_DIM), 0.1),
        "lambda_k2": nrm(ks[6], (DEPTH, HEAD_DIM), 0.1),
        "subln_w": 1.0 + nrm(ks[7], (DEPTH, 2 * HEAD_DIM), 0.02),
        "sinks": nrm(ks[8], (DEPTH, SWA_Q_HEADS), 0.5),
        "w_out": nrm(ks[9], (DEPTH, MIX_WIDTH, D_MODEL), MIX_WIDTH ** -0.5),
        "ffn_norm_w": 1.0 + nrm(ks[10], (DEPTH, D_MODEL), 0.02),
        "w_gate": nrm(ks[11], (DEPTH, D_MODEL, D_FF), D_MODEL ** -0.5),
        "w_up": nrm(ks[12], (DEPTH, D_MODEL, D_FF), D_MODEL ** -0.5),
        "w_down": nrm(ks[13], (DEPTH, D_FF, D_MODEL), D_FF ** -0.5),
        "final_norm_w": 1.0 + nrm(ks[14], (D_MODEL,), 0.02),
    }


def reference(x, attn_norm_w, w_in, lambda_q1, lambda_k1, lambda_q2, lambda_k2, subln_w,
              sinks, w_out, ffn_norm_w, w_gate, w_up, w_down, final_norm_w):
    B, S = x.shape[0], x.shape[1]
    splits = np.cumsum([DIFF_Q_COLS, DIFF_K_COLS, DIFF_V_COLS, SWA_Q_COLS, SWA_K_COLS]).tolist()
    for l in range(DEPTH):
        lambda_init = lambda_init_fn(l)
        h = rmsnorm(x, attn_norm_w[l])
        proj = jnp.einsum('bsd,dc->bsc', h, w_in[l])
        qa, ka, va, qs, ksw, vs = jnp.split(proj, splits, axis=-1)
        oa = diff_attention(
            qa.reshape(B, S, DIFF_HEADS, 2, HEAD_DIM),
            ka.reshape(B, S, DIFF_HEADS, 2, HEAD_DIM),
            va.reshape(B, S, DIFF_HEADS, 2 * HEAD_DIM),
            lambda_q1[l], lambda_k1[l], lambda_q2[l], lambda_k2[l], subln_w[l], lambda_init)
        ob = swa_sink_attention(
            qs.reshape(B, S, SWA_Q_HEADS, HEAD_DIM),
            ksw.reshape(B, S, SWA_KV_HEADS, HEAD_DIM),
            vs.reshape(B, S, SWA_KV_HEADS, HEAD_DIM),
            sinks[l])
        mixed = jnp.concatenate([oa, ob], axis=-1)
        x = x + jnp.einsum('bsc,cd->bsd', mixed, w_out[l])
        h = rmsnorm(x, ffn_norm_w[l])
        g = jnp.einsum('bsd,df->bsf', h, w_gate[l])
        u = jnp.einsum('bsd,df->bsf', h, w_up[l])
        x = x + jnp.einsum('bsf,fd->bsd', jax.nn.silu(g) * u, w_down[l])
    return rmsnorm(x, final_norm_w)
```

```python
import functools
import math

import jax
import jax.numpy as jnp
import numpy as np
from jax import lax
from jax.experimental import pallas as pl
from jax.experimental.pallas import tpu as pltpu

D_MODEL = 2048
HEAD_DIM = 64
DIFF_HEADS = 8
DIFF_WIDTH = 1024
SWA_Q_HEADS = 16
SWA_KV_HEADS = 4
SWA_GROUP = 4
SWA_WIDTH = 1024
WINDOW = 128
D_FF = 5632
IN_COLS = 4608
RMS_EPS = 1e-5
LOG2E = math.log2(math.e)
Q_SCALE = HEAD_DIM ** -0.5 * LOG2E
NEG_INF = float("-inf")

V7X_VMEM_BYTES = 64 * 1024 * 1024
VMEM_LIMIT_BYTES = 56 * 1024 * 1024

F32 = jnp.float32
BF16 = jnp.bfloat16


def _alibi_slopes(n_heads):
    return np.array([2.0 ** (-8.0 * (h + 1) / n_heads) for h in range(n_heads)], dtype=np.float64)


def _rms_scale(x):
    return x * lax.rsqrt(jnp.mean(x * x, axis=-1, keepdims=True) + RMS_EPS)


def _norm_inproj_kernel(x_ref, nw_ref, cs_ref, w_ref, o_ref, h_ref):
    @pl.when(pl.program_id(1) == 0)
    def _():
        h_ref[...] = (_rms_scale(x_ref[...]) * nw_ref[...]).astype(BF16)

    acc = jnp.dot(h_ref[...], w_ref[...], preferred_element_type=F32)
    o_ref[...] = (acc * cs_ref[...]).astype(o_ref.dtype)


def _norm_inproj(x2, norm_w, col_scale, w_bf16, *, tm, tn):
    tokens, d = x2.shape
    n = w_bf16.shape[1]
    return pl.pallas_call(
        _norm_inproj_kernel,
        out_shape=jax.ShapeDtypeStruct((tokens, n), BF16),
        grid=(tokens // tm, n // tn),
        in_specs=[
            pl.BlockSpec((tm, d), lambda i, j: (i, 0)),
            pl.BlockSpec((1, d), lambda i, j: (0, 0)),
            pl.BlockSpec((1, tn), lambda i, j: (0, j)),
            pl.BlockSpec((d, tn), lambda i, j: (0, j)),
        ],
        out_specs=pl.BlockSpec((tm, tn), lambda i, j: (i, j)),
        scratch_shapes=[pltpu.VMEM((tm, d), BF16)],
        compiler_params=pltpu.CompilerParams(
            dimension_semantics=("parallel", "arbitrary"),
            vmem_limit_bytes=VMEM_LIMIT_BYTES,
        ),
        name="norm_inproj",
    )(x2, norm_w, col_scale, w_bf16)


def _diff_attn_kernel(slope_ref, lq1_ref, lk1_ref, lq2_ref, lk2_ref, q_ref, k_ref, v_ref, sw_ref,
                      o_ref, m_ref, l_ref, acc_ref, *, tq, lambda_init):
    head = pl.program_id(1)
    qi = pl.program_id(2)
    slope2 = slope_ref[head]
    tk = tq

    q = q_ref[...]
    lane = lax.broadcasted_iota(jnp.int32, q.shape, 1)
    zero = jnp.zeros_like(q)
    qz = jnp.concatenate([jnp.where(lane < HEAD_DIM, q, zero), jnp.where(lane >= HEAD_DIM, q, zero)], axis=0)

    m_ref[...] = jnp.full(m_ref.shape, NEG_INF, F32)
    l_ref[...] = jnp.zeros(l_ref.shape, F32)
    acc_ref[...] = jnp.zeros(acc_ref.shape, F32)

    kcol = lax.broadcasted_iota(jnp.int32, (1, tk), 1)

    def step(j, masked):
        k0 = pl.multiple_of(j * tk, tk)
        k = k_ref[pl.ds(k0, tk), :]
        v = v_ref[pl.ds(k0, tk), :]
        s = lax.dot_general(qz, k, (((1,), (1,)), ((), ())), preferred_element_type=F32)
        s = s + slope2 * (kcol + (j - qi) * tk).astype(F32)
        if masked:
            row = lax.broadcasted_iota(jnp.int32, (tq, tk), 0)
            col = lax.broadcasted_iota(jnp.int32, (tq, tk), 1)
            keep = col <= row
            keep = jnp.concatenate([keep, keep], axis=0)
            s = jnp.where(keep, s, NEG_INF)
        m_prev = m_ref[...]
        m_new = jnp.maximum(m_prev, jnp.max(s, axis=1, keepdims=True))
        alpha = jnp.exp2(m_prev - m_new)
        p = jnp.exp2(s - m_new)
        l_ref[...] = alpha * l_ref[...] + jnp.sum(p, axis=1, keepdims=True)
        acc_ref[...] = alpha * acc_ref[...] + jnp.dot(p.astype(BF16), v, preferred_element_type=F32)
        m_ref[...] = m_new

    def body(j, carry):
        step(j, False)
        return carry

    lax.fori_loop(0, qi, body, 0)
    step(qi, True)

    acc = acc_ref[...]
    l = l_ref[...]
    o1 = acc[:tq] / l[:tq]
    o2 = acc[tq:] / l[tq:]
    lam = (jnp.exp(jnp.sum(lq1_ref[...] * lk1_ref[...], axis=-1, keepdims=True))
           - jnp.exp(jnp.sum(lq2_ref[...] * lk2_ref[...], axis=-1, keepdims=True))
           + lambda_init)
    o = o1 - lam * o2
    o = _rms_scale(o) * sw_ref[...] * (1.0 - lambda_init)
    o_ref[...] = o.astype(o_ref.dtype)


def _diff_attention(proj, slopes2, lq1, lk1, lq2, lk2, subln_w, *, batch, seq, tq, lambda_init):
    nq = seq // tq
    lane_blk = 2 * HEAD_DIM
    k_blk0 = DIFF_WIDTH // lane_blk
    v_blk0 = 2 * DIFF_WIDTH // lane_blk
    vec = lambda: pl.BlockSpec((1, HEAD_DIM), lambda b, h, i: (0, 0))
    return pl.pallas_call(
        functools.partial(_diff_attn_kernel, tq=tq, lambda_init=lambda_init),
        out_shape=jax.ShapeDtypeStruct((batch * seq, DIFF_WIDTH), BF16),
        grid=(batch, DIFF_HEADS, nq),
        in_specs=[
            pl.BlockSpec(memory_space=pltpu.SMEM),
            vec(), vec(), vec(), vec(),
            pl.BlockSpec((tq, lane_blk), lambda b, h, i: (b * nq + i, h)),
            pl.BlockSpec((seq, lane_blk), lambda b, h, i: (b, k_blk0 + h)),
            pl.BlockSpec((seq, lane_blk), lambda b, h, i: (b, v_blk0 + h)),
            pl.BlockSpec((1, lane_blk), lambda b, h, i: (0, 0)),
        ],
        out_specs=pl.BlockSpec((tq, lane_blk), lambda b, h, i: (b * nq + i, h)),
        scratch_shapes=[
            pltpu.VMEM((2 * tq, 1), F32),
            pltpu.VMEM((2 * tq, 1), F32),
            pltpu.VMEM((2 * tq, lane_blk), F32),
        ],
        compiler_params=pltpu.CompilerParams(
            dimension_semantics=("parallel", "parallel", "arbitrary"),
            vmem_limit_bytes=VMEM_LIMIT_BYTES,
        ),
        name="diff_attention",
    )(slopes2, lq1, lk1, lq2, lk2, proj, proj, proj, subln_w)


def _swa_tables():
    slopes = _alibi_slopes(SWA_Q_HEADS)
    i = np.arange(WINDOW)[:, None]
    j = np.arange(2 * WINDOW)[None, :]
    dist = WINDOW + i - j
    valid = (dist >= 0) & (dist < WINDOW)
    tbl = np.empty((SWA_KV_HEADS // 2, SWA_GROUP * 2 * WINDOW, 2 * WINDOW), np.float32)
    for pair in range(SWA_KV_HEADS // 2):
        for g in range(SWA_GROUP):
            for e in range(2):
                head = (2 * pair + e) * SWA_GROUP + g
                r0 = (g * 2 + e) * WINDOW
                tbl[pair, r0:r0 + WINDOW] = np.where(valid, -slopes[head] * LOG2E * dist, -np.inf)
    return tbl


def _swa_kernel(sink_ref, tbl_ref, q_ref, k_ref, v_ref, o_ref, *, tq):
    qi = pl.program_id(1)
    n_sub = tq // WINDOW
    lane_q = lax.broadcasted_iota(jnp.int32, (WINDOW, 2 * HEAD_DIM), 1)
    lane_kv = lax.broadcasted_iota(jnp.int32, (2 * WINDOW, 2 * HEAD_DIM), 1)
    kcol = lax.broadcasted_iota(jnp.int32, (1, 2 * WINDOW), 1)
    for sub in range(n_sub):
        r0 = sub * WINDOW
        blk = qi * n_sub + sub
        cur0 = pl.multiple_of(blk * WINDOW, WINDOW)
        prev0 = pl.multiple_of(jnp.maximum(blk - 1, 0) * WINDOW, WINDOW)
        for pair in range(SWA_KV_HEADS // 2):
            c0 = pair * 2 * HEAD_DIM
            k2 = jnp.concatenate([k_ref[pl.ds(prev0, WINDOW), c0:c0 + 2 * HEAD_DIM],
                                  k_ref[pl.ds(cur0, WINDOW), c0:c0 + 2 * HEAD_DIM]], axis=0)
            v2 = jnp.concatenate([v_ref[pl.ds(prev0, WINDOW), c0:c0 + 2 * HEAD_DIM],
                                  v_ref[pl.ds(cur0, WINDOW), c0:c0 + 2 * HEAD_DIM]], axis=0)
            zkv = jnp.zeros_like(v2)
            v_even = jnp.where(lane_kv < HEAD_DIM, v2, zkv)
            v_odd = jnp.where(lane_kv >= HEAD_DIM, v2, zkv)
            rows = []
            sink_rows = []
            for g in range(SWA_GROUP):
                t0 = (pair * SWA_GROUP + g) * 2 * HEAD_DIM
                qt = q_ref[r0:r0 + WINDOW, t0:t0 + 2 * HEAD_DIM]
                zq = jnp.zeros_like(qt)
                rows.append(jnp.where(lane_q < HEAD_DIM, qt, zq))
                rows.append(jnp.where(lane_q >= HEAD_DIM, qt, zq))
                for e in range(2):
                    sink_rows.append(jnp.full((WINDOW, 1), sink_ref[(2 * pair + e) * SWA_GROUP + g] * LOG2E, F32))
            q8 = jnp.concatenate(rows, axis=0)
            sink = jnp.concatenate(sink_rows, axis=0)
            s = lax.dot_general(q8, k2, (((1,), (1,)), ((), ())), preferred_element_type=F32)
            s = s + tbl_ref[pair]
            if sub == 0:
                s = s + jnp.where((kcol < WINDOW) & (blk == 0), NEG_INF, 0.0)
            m = jnp.maximum(jnp.max(s, axis=1, keepdims=True), sink)
            p = jnp.exp2(s - m)
            l = jnp.sum(p, axis=1, keepdims=True) + jnp.exp2(sink - m)
            pb = p.astype(BF16)
            inv = 1.0 / l
            for g in range(SWA_GROUP):
                ra = (g * 2) * WINDOW
                rb = (g * 2 + 1) * WINDOW
                oe = jnp.dot(pb[ra:ra + WINDOW], v_even, preferred_element_type=F32)
                oo = jnp.dot(pb[rb:rb + WINDOW], v_odd, preferred_element_type=F32)
                o = oe * inv[ra:ra + WINDOW] + oo * inv[rb:rb + WINDOW]
                t0 = (pair * SWA_GROUP + g) * 2 * HEAD_DIM
                o_ref[r0:r0 + WINDOW, t0:t0 + 2 * HEAD_DIM] = o.astype(o_ref.dtype)


def _swa_attention(proj, sinks, tbl, *, batch, seq, tq):
    nq = seq // tq
    q_blk = (3 * DIFF_WIDTH) // SWA_WIDTH
    kv_w = SWA_KV_HEADS * HEAD_DIM
    k_blk = (3 * DIFF_WIDTH + SWA_WIDTH) // kv_w
    return pl.pallas_call(
        functools.partial(_swa_kernel, tq=tq),
        out_shape=jax.ShapeDtypeStruct((batch * seq, SWA_WIDTH), BF16),
        grid=(batch, nq),
        in_specs=[
            pl.BlockSpec(memory_space=pltpu.SMEM),
            pl.BlockSpec(tbl.shape, lambda b, i: (0, 0, 0)),
            pl.BlockSpec((tq, SWA_WIDTH), lambda b, i: (b * nq + i, q_blk)),
            pl.BlockSpec((seq, kv_w), lambda b, i: (b, k_blk)),
            pl.BlockSpec((seq, kv_w), lambda b, i: (b, k_blk + 1)),
        ],
        out_specs=pl.BlockSpec((tq, SWA_WIDTH), lambda b, i: (b * nq + i, 0)),
        compiler_params=pltpu.CompilerParams(
            dimension_semantics=("parallel", "arbitrary"),
            vmem_limit_bytes=VMEM_LIMIT_BYTES,
        ),
        name="swa_attention",
    )(sinks, tbl, proj, proj, proj)


def _outproj_kernel(x_ref, oa_ref, ob_ref, wa_ref, wb_ref, nw_ref, x1_ref, h_ref):
    y = jnp.dot(oa_ref[...], wa_ref[...], preferred_element_type=F32)
    y = y + jnp.dot(ob_ref[...], wb_ref[...], preferred_element_type=F32)
    x1 = x_ref[...] + y
    x1_ref[...] = x1
    h_ref[...] = (_rms_scale(x1) * nw_ref[...]).astype(h_ref.dtype)


def _outproj(x2, oa, ob, wa, wb, norm_w, *, tm):
    tokens, d = x2.shape
    return pl.pallas_call(
        _outproj_kernel,
        out_shape=(jax.ShapeDtypeStruct((tokens, d), F32), jax.ShapeDtypeStruct((tokens, d), BF16)),
        grid=(tokens // tm,),
        in_specs=[
            pl.BlockSpec((tm, d), lambda i: (i, 0)),
            pl.BlockSpec((tm, DIFF_WIDTH), lambda i: (i, 0)),
            pl.BlockSpec((tm, SWA_WIDTH), lambda i: (i, 0)),
            pl.BlockSpec((DIFF_WIDTH, d), lambda i: (0, 0)),
            pl.BlockSpec((SWA_WIDTH, d), lambda i: (0, 0)),
            pl.BlockSpec((1, d), lambda i: (0, 0)),
        ],
        out_specs=(pl.BlockSpec((tm, d), lambda i: (i, 0)), pl.BlockSpec((tm, d), lambda i: (i, 0))),
        compiler_params=pltpu.CompilerParams(
            dimension_semantics=("parallel",),
            vmem_limit_bytes=VMEM_LIMIT_BYTES,
        ),
        name="outproj_residual_norm",
    )(x2, oa, ob, wa, wb, norm_w)


def _ffn_kernel(h_ref, x1_ref, wg_ref, wu_ref, wd_ref, nw_ref, o_ref):
    f = pl.program_id(1)
    h = h_ref[...]
    g = jnp.dot(h, wg_ref[...], preferred_element_type=F32)
    u = jnp.dot(h, wu_ref[...], preferred_element_type=F32)
    a = (g * (1.0 / (1.0 + jnp.exp(-g))) * u).astype(BF16)
    y = jnp.dot(a, wd_ref[...], preferred_element_type=F32)

    @pl.when(f == 0)
    def _():
        o_ref[...] = x1_ref[...] + y

    @pl.when(f > 0)
    def _():
        o_ref[...] += y

    @pl.when(f == pl.num_programs(1) - 1)
    def _():
        o_ref[...] = _rms_scale(o_ref[...]) * nw_ref[...]


def _ffn(h2, x1, wg, wu, wd, norm_w, *, tm, tf):
    tokens, d = x1.shape
    dff = wg.shape[1]
    return pl.pallas_call(
        _ffn_kernel,
        out_shape=jax.ShapeDtypeStruct((tokens, d), F32),
        grid=(tokens // tm, dff // tf),
        in_specs=[
            pl.BlockSpec((tm, d), lambda i, f: (i, 0)),
            pl.BlockSpec((tm, d), lambda i, f: (i, 0)),
            pl.BlockSpec((d, tf), lambda i, f: (0, f)),
            pl.BlockSpec((d, tf), lambda i, f: (0, f)),
            pl.BlockSpec((tf, d), lambda i, f: (f, 0)),
            pl.BlockSpec((1, d), lambda i, f: (0, 0)),
        ],
        out_specs=pl.BlockSpec((tm, d), lambda i, f: (i, 0)),
        compiler_params=pltpu.CompilerParams(
            dimension_semantics=("parallel", "arbitrary"),
            vmem_limit_bytes=VMEM_LIMIT_BYTES,
        ),
        name="swiglu_ffn_final_norm",
    )(h2, x1, wg, wu, wd, norm_w)


def kernel(x, attn_norm_w, w_in, lambda_q1, lambda_k1, lambda_q2, lambda_k2, subln_w, sinks, w_out,
           ffn_norm_w, w_gate, w_up, w_down, final_norm_w):
    batch, seq, d = x.shape
    depth = w_in.shape[0]
    assert (d, w_in.shape[2], w_gate.shape[2]) == (D_MODEL, IN_COLS, D_FF)
    tokens = batch * seq
    x2 = x.reshape(tokens, d)

    q_cols = np.ones((1, IN_COLS), np.float32)
    q_cols[:, :DIFF_WIDTH] = Q_SCALE
    q_cols[:, 3 * DIFF_WIDTH:3 * DIFF_WIDTH + SWA_WIDTH] = Q_SCALE
    col_scale = jnp.asarray(q_cols)
    diff_slopes2 = jnp.asarray((_alibi_slopes(DIFF_HEADS) * LOG2E).astype(np.float32))
    swa_tbl = jnp.asarray(_swa_tables())
    n_pair = SWA_KV_HEADS // 2

    assert depth == 1
    l = 0
    lambda_init = 0.8 - 0.6 * math.exp(-0.3 * l)
    wq = w_in[l][:, 3 * DIFF_WIDTH:3 * DIFF_WIDTH + SWA_WIDTH]
    wq = wq.reshape(d, n_pair, 2, SWA_GROUP, HEAD_DIM).transpose(0, 1, 3, 2, 4).reshape(d, SWA_WIDTH)
    w_in_b = jnp.concatenate(
        [w_in[l][:, :3 * DIFF_WIDTH], wq, w_in[l][:, 3 * DIFF_WIDTH + SWA_WIDTH:]], axis=1).astype(BF16)
    wo_a = w_out[l][:DIFF_WIDTH].astype(BF16)
    wo_b = w_out[l][DIFF_WIDTH:].reshape(n_pair, 2, SWA_GROUP, HEAD_DIM, d).transpose(0, 2, 1, 3, 4)
    wo_b = wo_b.reshape(SWA_WIDTH, d).astype(BF16)

    proj = _norm_inproj(x2, attn_norm_w[l].reshape(1, d), col_scale, w_in_b, tm=1024, tn=512)
    oa = _diff_attention(
        proj, diff_slopes2,
        lambda_q1[l].reshape(1, HEAD_DIM), lambda_k1[l].reshape(1, HEAD_DIM),
        lambda_q2[l].reshape(1, HEAD_DIM), lambda_k2[l].reshape(1, HEAD_DIM),
        subln_w[l].reshape(1, 2 * HEAD_DIM),
        batch=batch, seq=seq, tq=256, lambda_init=lambda_init)
    ob = _swa_attention(proj, sinks[l], swa_tbl, batch=batch, seq=seq, tq=512)
    x1, h2 = _outproj(x2, oa, ob, wo_a, wo_b, ffn_norm_w[l].reshape(1, d), tm=512)
    out = _ffn(h2, x1, w_gate[l].astype(BF16), w_up[l].astype(BF16), w_down[l].astype(BF16),
               final_norm_w.reshape(1, d), tm=512, tf=512)
    return out.reshape(batch, seq, d)
```

```python
import functools
import math

import jax
import jax.numpy as jnp
import numpy as np
from jax import lax
from jax.experimental import pallas as pl
from jax.experimental.pallas import tpu as pltpu

D_MODEL = 2048
HEAD_DIM = 64
DIFF_HEADS = 8
DIFF_WIDTH = 1024
SWA_Q_HEADS = 16
SWA_KV_HEADS = 4
SWA_GROUP = 4
SWA_WIDTH = 1024
WINDOW = 128
D_FF = 5632
IN_COLS = 4608
RMS_EPS = 1e-5
LOG2E = math.log2(math.e)
Q_SCALE = HEAD_DIM ** -0.5 * LOG2E
NEG_INF = float("-inf")

V7X_VMEM_BYTES = 64 * 1024 * 1024
VMEM_LIMIT_BYTES = 56 * 1024 * 1024

F32 = jnp.float32
BF16 = jnp.bfloat16


def _alibi_slopes(n_heads):
    return np.array([2.0 ** (-8.0 * (h + 1) / n_heads) for h in range(n_heads)], dtype=np.float64)


def _rms_scale(x):
    return x * lax.rsqrt(jnp.mean(x * x, axis=-1, keepdims=True) + RMS_EPS)


def _norm_inproj_kernel(x_ref, nw_ref, cs_ref, w_ref, o_ref, h_ref):
    @pl.when(pl.program_id(1) == 0)
    def _():
        h_ref[...] = (_rms_scale(x_ref[...]) * nw_ref[...]).astype(BF16)

    acc = jnp.dot(h_ref[...], w_ref[...], preferred_element_type=F32)
    o_ref[...] = (acc * cs_ref[...]).astype(o_ref.dtype)


def _norm_inproj(x2, norm_w, col_scale, w_bf16, *, tm, tn):
    tokens, d = x2.shape
    n = w_bf16.shape[1]
    return pl.pallas_call(
        _norm_inproj_kernel,
        out_shape=jax.ShapeDtypeStruct((tokens, n), BF16),
        grid=(tokens // tm, n // tn),
        in_specs=[
            pl.BlockSpec((tm, d), lambda i, j: (i, 0)),
            pl.BlockSpec((1, d), lambda i, j: (0, 0)),
            pl.BlockSpec((1, tn), lambda i, j: (0, j)),
            pl.BlockSpec((d, tn), lambda i, j: (0, j)),
        ],
        out_specs=pl.BlockSpec((tm, tn), lambda i, j: (i, j)),
        scratch_shapes=[pltpu.VMEM((tm, d), BF16)],
        compiler_params=pltpu.CompilerParams(
            dimension_semantics=("parallel", "arbitrary"),
            vmem_limit_bytes=VMEM_LIMIT_BYTES,
        ),
        name="norm_inproj",
    )(x2, norm_w, col_scale, w_bf16)


def _diff_attn_kernel(slope_ref, lq1_ref, lk1_ref, lq2_ref, lk2_ref, q_ref, k_ref, v_ref, sw_ref,
                      o_ref, qz_ref, m_ref, acc_ref, *, tq, rc, lambda_init):
    head = pl.program_id(1)
    qi = pl.program_id(2)
    slope2 = slope_ref[head]
    tk = tq
    lanes = 2 * HEAD_DIM
    chunks_per_map = tq // rc
    nt = (((1,), (1,)), ((), ()))

    q = q_ref[...]
    lane = lax.broadcasted_iota(jnp.int32, q.shape, 1)
    zero = jnp.zeros_like(q)
    qz_ref[:tq, :] = jnp.where(lane < HEAD_DIM, q, zero)
    qz_ref[tq:, :] = jnp.where(lane >= HEAD_DIM, q, zero)
    m_ref[...] = jnp.full(m_ref.shape, NEG_INF, F32)
    acc_ref[...] = jnp.zeros(acc_ref.shape, F32)

    kcol = lax.broadcasted_iota(jnp.int32, (1, tk), 1)
    ones = jnp.ones((tk, lanes), BF16)
    tri = lax.broadcasted_iota(jnp.int32, (rc, rc), 1) <= lax.broadcasted_iota(jnp.int32, (rc, rc), 0)

    def chunk_update(c, s, vaug):
        r0 = c * rc
        m_prev = m_ref[r0:r0 + rc, :]
        m_new = jnp.maximum(m_prev, jnp.max(s, axis=1, keepdims=True))
        alpha = jnp.exp2(m_prev - m_new)
        p = jnp.exp2(s - jnp.concatenate([m_new] * (s.shape[1] // lanes), axis=1))
        pv = jnp.dot(p.astype(BF16), vaug, preferred_element_type=F32)
        acc_ref[r0:r0 + rc, :] = jnp.concatenate([alpha, alpha], axis=1) * acc_ref[r0:r0 + rc, :] + pv
        m_ref[r0:r0 + rc, :] = m_new

    def full_step(j, carry):
        k0 = pl.multiple_of(j * tk, tk)
        k = k_ref[pl.ds(k0, tk), :]
        vaug = jnp.concatenate([v_ref[pl.ds(k0, tk), :], ones], axis=1)
        bias = slope2 * (kcol + (j - qi) * tk).astype(F32)
        for c in range(2 * chunks_per_map):
            s = lax.dot_general(qz_ref[c * rc:(c + 1) * rc, :], k, nt, preferred_element_type=F32) + bias
            chunk_update(c, s, vaug)
        return carry

    lax.fori_loop(0, qi, full_step, 0)

    k0 = pl.multiple_of(qi * tk, tk)
    bias = slope2 * kcol.astype(F32)
    for c in range(2 * chunks_per_map):
        a = (c % chunks_per_map) * rc
        ncol = a + rc
        k = k_ref[pl.ds(k0, ncol), :]
        vaug = jnp.concatenate([v_ref[pl.ds(k0, ncol), :], ones[:ncol]], axis=1)
        s = lax.dot_general(qz_ref[c * rc:(c + 1) * rc, :], k, nt, preferred_element_type=F32) + bias[:, :ncol]
        s_diag = jnp.where(tri, s[:, a:], NEG_INF)
        s = jnp.concatenate([s[:, :a], s_diag], axis=1) if a else s_diag
        chunk_update(c, s, vaug)

    lam = (jnp.exp(jnp.sum(lq1_ref[...] * lk1_ref[...], axis=-1, keepdims=True))
           - jnp.exp(jnp.sum(lq2_ref[...] * lk2_ref[...], axis=-1, keepdims=True))
           + lambda_init)
    for c in range(chunks_per_map):
        r1 = c * rc
        r2 = tq + c * rc
        o1 = acc_ref[r1:r1 + rc, :lanes] / acc_ref[r1:r1 + rc, lanes:]
        o2 = acc_ref[r2:r2 + rc, :lanes] / acc_ref[r2:r2 + rc, lanes:]
        o = o1 - lam * o2
        o = _rms_scale(o) * sw_ref[...] * (1.0 - lambda_init)
        o_ref[r1:r1 + rc, :] = o.astype(o_ref.dtype)


def _diff_attention(proj, slopes2, lq1, lk1, lq2, lk2, subln_w, *, batch, seq, tq, rc, lambda_init):
    nq = seq // tq
    lane_blk = 2 * HEAD_DIM
    k_blk0 = DIFF_WIDTH // lane_blk
    v_blk0 = 2 * DIFF_WIDTH // lane_blk
    vec = lambda: pl.BlockSpec((1, HEAD_DIM), lambda b, h, i: (0, 0))
    return pl.pallas_call(
        functools.partial(_diff_attn_kernel, tq=tq, rc=rc, lambda_init=lambda_init),
        out_shape=jax.ShapeDtypeStruct((batch * seq, DIFF_WIDTH), BF16),
        grid=(batch, DIFF_HEADS, nq),
        in_specs=[
            pl.BlockSpec(memory_space=pltpu.SMEM),
            vec(), vec(), vec(), vec(),
            pl.BlockSpec((tq, lane_blk), lambda b, h, i: (b * nq + i, h)),
            pl.BlockSpec((seq, lane_blk), lambda b, h, i: (b, k_blk0 + h)),
            pl.BlockSpec((seq, lane_blk), lambda b, h, i: (b, v_blk0 + h)),
            pl.BlockSpec((1, lane_blk), lambda b, h, i: (0, 0)),
        ],
        out_specs=pl.BlockSpec((tq, lane_blk), lambda b, h, i: (b * nq + i, h)),
        scratch_shapes=[
            pltpu.VMEM((2 * tq, lane_blk), BF16),
            pltpu.VMEM((2 * tq, lane_blk), F32),
            pltpu.VMEM((2 * tq, 2 * lane_blk), F32),
        ],
        compiler_params=pltpu.CompilerParams(
            dimension_semantics=("parallel", "parallel", "arbitrary"),
            vmem_limit_bytes=VMEM_LIMIT_BYTES,
        ),
        name="diff_attention",
    )(slopes2, lq1, lk1, lq2, lk2, proj, proj, proj, subln_w)


def _swa_tables():
    slopes = _alibi_slopes(SWA_Q_HEADS)
    i = np.arange(WINDOW)[:, None]
    j = np.arange(2 * WINDOW)[None, :]
    dist = WINDOW + i - j
    valid = (dist >= 0) & (dist < WINDOW)
    tbl = np.empty((SWA_KV_HEADS // 2, SWA_GROUP * 2 * WINDOW, 2 * WINDOW), np.float32)
    for pair in range(SWA_KV_HEADS // 2):
        for g in range(SWA_GROUP):
            for e in range(2):
                head = (2 * pair + e) * SWA_GROUP + g
                r0 = (g * 2 + e) * WINDOW
                tbl[pair, r0:r0 + WINDOW] = np.where(valid, -slopes[head] * LOG2E * dist, -np.inf)
    return tbl


def _swa_kernel(sink_ref, tbl_ref, q_ref, k_ref, v_ref, o_ref, *, tq):
    qi = pl.program_id(1)
    n_sub = tq // WINDOW
    lane_q = lax.broadcasted_iota(jnp.int32, (WINDOW, 2 * HEAD_DIM), 1)
    lane_kv = lax.broadcasted_iota(jnp.int32, (2 * WINDOW, 2 * HEAD_DIM), 1)
    kcol = lax.broadcasted_iota(jnp.int32, (1, 2 * WINDOW), 1)
    for sub in range(n_sub):
        r0 = sub * WINDOW
        blk = qi * n_sub + sub
        cur0 = pl.multiple_of(blk * WINDOW, WINDOW)
        prev0 = pl.multiple_of(jnp.maximum(blk - 1, 0) * WINDOW, WINDOW)
        for pair in range(SWA_KV_HEADS // 2):
            c0 = pair * 2 * HEAD_DIM
            k2 = jnp.concatenate([k_ref[pl.ds(prev0, WINDOW), c0:c0 + 2 * HEAD_DIM],
                                  k_ref[pl.ds(cur0, WINDOW), c0:c0 + 2 * HEAD_DIM]], axis=0)
            v2 = jnp.concatenate([v_ref[pl.ds(prev0, WINDOW), c0:c0 + 2 * HEAD_DIM],
                                  v_ref[pl.ds(cur0, WINDOW), c0:c0 + 2 * HEAD_DIM]], axis=0)
            zkv = jnp.zeros_like(v2)
            v_even = jnp.where(lane_kv < HEAD_DIM, v2, zkv)
            v_odd = jnp.where(lane_kv >= HEAD_DIM, v2, zkv)
            rows = []
            sink_rows = []
            for g in range(SWA_GROUP):
                t0 = (pair * SWA_GROUP + g) * 2 * HEAD_DIM
                qt = q_ref[r0:r0 + WINDOW, t0:t0 + 2 * HEAD_DIM]
                zq = jnp.zeros_like(qt)
                rows.append(jnp.where(lane_q < HEAD_DIM, qt, zq))
                rows.append(jnp.where(lane_q >= HEAD_DIM, qt, zq))
                for e in range(2):
                    sink_rows.append(jnp.full((WINDOW, 1), sink_ref[(2 * pair + e) * SWA_GROUP + g] * LOG2E, F32))
            q8 = jnp.concatenate(rows, axis=0)
            sink = jnp.concatenate(sink_rows, axis=0)
            s = lax.dot_general(q8, k2, (((1,), (1,)), ((), ())), preferred_element_type=F32)
            s = s + tbl_ref[pair]
            if sub == 0:
                s = s + jnp.where((kcol < WINDOW) & (blk == 0), NEG_INF, 0.0)
            m = jnp.maximum(jnp.max(s, axis=1, keepdims=True), sink)
            p = jnp.exp2(s - m)
            l = jnp.sum(p, axis=1, keepdims=True) + jnp.exp2(sink - m)
            pb = p.astype(BF16)
            inv = 1.0 / l
            for g in range(SWA_GROUP):
                ra = (g * 2) * WINDOW
                rb = (g * 2 + 1) * WINDOW
                oe = jnp.dot(pb[ra:ra + WINDOW], v_even, preferred_element_type=F32)
                oo = jnp.dot(pb[rb:rb + WINDOW], v_odd, preferred_element_type=F32)
                o = oe * inv[ra:ra + WINDOW] + oo * inv[rb:rb + WINDOW]
                t0 = (pair * SWA_GROUP + g) * 2 * HEAD_DIM
                o_ref[r0:r0 + WINDOW, t0:t0 + 2 * HEAD_DIM] = o.astype(o_ref.dtype)


def _swa_attention(proj, sinks, tbl, *, batch, seq, tq):
    nq = seq // tq
    q_blk = (3 * DIFF_WIDTH) // SWA_WIDTH
    kv_w = SWA_KV_HEADS * HEAD_DIM
    k_blk = (3 * DIFF_WIDTH + SWA_WIDTH) // kv_w
    return pl.pallas_call(
        functools.partial(_swa_kernel, tq=tq),
        out_shape=jax.ShapeDtypeStruct((batch * seq, SWA_WIDTH), BF16),
        grid=(batch, nq),
        in_specs=[
            pl.BlockSpec(memory_space=pltpu.SMEM),
            pl.BlockSpec(tbl.shape, lambda b, i: (0, 0, 0)),
            pl.BlockSpec((tq, SWA_WIDTH), lambda b, i: (b * nq + i, q_blk)),
            pl.BlockSpec((seq, kv_w), lambda b, i: (b, k_blk)),
            pl.BlockSpec((seq, kv_w), lambda b, i: (b, k_blk + 1)),
        ],
        out_specs=pl.BlockSpec((tq, SWA_WIDTH), lambda b, i: (b * nq + i, 0)),
        compiler_params=pltpu.CompilerParams(
            dimension_semantics=("parallel", "arbitrary"),
            vmem_limit_bytes=VMEM_LIMIT_BYTES,
        ),
        name="swa_attention",
    )(sinks, tbl, proj, proj, proj)


def _outproj_kernel(x_ref, oa_ref, ob_ref, wa_ref, wb_ref, nw_ref, x1_ref, h_ref):
    y = jnp.dot(oa_ref[...], wa_ref[...], preferred_element_type=F32)
    y = y + jnp.dot(ob_ref[...], wb_ref[...], preferred_element_type=F32)
    x1 = x_ref[...] + y
    x1_ref[...] = x1
    h_ref[...] = (_rms_scale(x1) * nw_ref[...]).astype(h_ref.dtype)


def _outproj(x2, oa, ob, wa, wb, norm_w, *, tm):
    tokens, d = x2.shape
    return pl.pallas_call(
        _outproj_kernel,
        out_shape=(jax.ShapeDtypeStruct((tokens, d), F32), jax.ShapeDtypeStruct((tokens, d), BF16)),
        grid=(tokens // tm,),
        in_specs=[
            pl.BlockSpec((tm, d), lambda i: (i, 0)),
            pl.BlockSpec((tm, DIFF_WIDTH), lambda i: (i, 0)),
            pl.BlockSpec((tm, SWA_WIDTH), lambda i: (i, 0)),
            pl.BlockSpec((DIFF_WIDTH, d), lambda i: (0, 0)),
            pl.BlockSpec((SWA_WIDTH, d), lambda i: (0, 0)),
            pl.BlockSpec((1, d), lambda i: (0, 0)),
        ],
        out_specs=(pl.BlockSpec((tm, d), lambda i: (i, 0)), pl.BlockSpec((tm, d), lambda i: (i, 0))),
        compiler_params=pltpu.CompilerParams(
            dimension_semantics=("parallel",),
            vmem_limit_bytes=VMEM_LIMIT_BYTES,
        ),
        name="outproj_residual_norm",
    )(x2, oa, ob, wa, wb, norm_w)


def _ffn_kernel(h_ref, x1_ref, wg_ref, wu_ref, wd_ref, nw_ref, o_ref):
    f = pl.program_id(1)
    h = h_ref[...]
    g = jnp.dot(h, wg_ref[...], preferred_element_type=F32)
    u = jnp.dot(h, wu_ref[...], preferred_element_type=F32)
    a = (g * (1.0 / (1.0 + jnp.exp(-g))) * u).astype(BF16)
    y = jnp.dot(a, wd_ref[...], preferred_element_type=F32)

    @pl.when(f == 0)
    def _():
        o_ref[...] = x1_ref[...] + y

    @pl.when(f > 0)
    def _():
        o_ref[...] += y

    @pl.when(f == pl.num_programs(1) - 1)
    def _():
        o_ref[...] = _rms_scale(o_ref[...]) * nw_ref[...]


def _ffn(h2, x1, wg, wu, wd, norm_w, *, tm, tf):
    tokens, d = x1.shape
    dff = wg.shape[1]
    return pl.pallas_call(
        _ffn_kernel,
        out_shape=jax.ShapeDtypeStruct((tokens, d), F32),
        grid=(tokens // tm, dff // tf),
        in_specs=[
            pl.BlockSpec((tm, d), lambda i, f: (i, 0)),
            pl.BlockSpec((tm, d), lambda i, f: (i, 0)),
            pl.BlockSpec((d, tf), lambda i, f: (0, f)),
            pl.BlockSpec((d, tf), lambda i, f: (0, f)),
            pl.BlockSpec((tf, d), lambda i, f: (f, 0)),
            pl.BlockSpec((1, d), lambda i, f: (0, 0)),
        ],
        out_specs=pl.BlockSpec((tm, d), lambda i, f: (i, 0)),
        compiler_params=pltpu.CompilerParams(
            dimension_semantics=("parallel", "arbitrary"),
            vmem_limit_bytes=VMEM_LIMIT_BYTES,
        ),
        name="swiglu_ffn_final_norm",
    )(h2, x1, wg, wu, wd, norm_w)


def kernel(x, attn_norm_w, w_in, lambda_q1, lambda_k1, lambda_q2, lambda_k2, subln_w, sinks, w_out,
           ffn_norm_w, w_gate, w_up, w_down, final_norm_w):
    batch, seq, d = x.shape
    depth = w_in.shape[0]
    assert (d, w_in.shape[2], w_gate.shape[2]) == (D_MODEL, IN_COLS, D_FF)
    tokens = batch * seq
    x2 = x.reshape(tokens, d)

    q_cols = np.ones((1, IN_COLS), np.float32)
    q_cols[:, :DIFF_WIDTH] = Q_SCALE
    q_cols[:, 3 * DIFF_WIDTH:3 * DIFF_WIDTH + SWA_WIDTH] = Q_SCALE
    col_scale = jnp.asarray(q_cols)
    diff_slopes2 = jnp.asarray((_alibi_slopes(DIFF_HEADS) * LOG2E).astype(np.float32))
    swa_tbl = jnp.asarray(_swa_tables())
    n_pair = SWA_KV_HEADS // 2

    assert depth == 1
    l = 0
    lambda_init = 0.8 - 0.6 * math.exp(-0.3 * l)
    wq = w_in[l][:, 3 * DIFF_WIDTH:3 * DIFF_WIDTH + SWA_WIDTH]
    wq = wq.reshape(d, n_pair, 2, SWA_GROUP, HEAD_DIM).transpose(0, 1, 3, 2, 4).reshape(d, SWA_WIDTH)
    w_in_b = jnp.concatenate(
        [w_in[l][:, :3 * DIFF_WIDTH], wq, w_in[l][:, 3 * DIFF_WIDTH + SWA_WIDTH:]], axis=1).astype(BF16)
    wo_a = w_out[l][:DIFF_WIDTH].astype(BF16)
    wo_b = w_out[l][DIFF_WIDTH:].reshape(n_pair, 2, SWA_GROUP, HEAD_DIM, d).transpose(0, 2, 1, 3, 4)
    wo_b = wo_b.reshape(SWA_WIDTH, d).astype(BF16)

    proj = _norm_inproj(x2, attn_norm_w[l].reshape(1, d), col_scale, w_in_b, tm=1024, tn=512)
    oa = _diff_attention(
        proj, diff_slopes2,
        lambda_q1[l].reshape(1, HEAD_DIM), lambda_k1[l].reshape(1, HEAD_DIM),
        lambda_q2[l].reshape(1, HEAD_DIM), lambda_k2[l].reshape(1, HEAD_DIM),
        subln_w[l].reshape(1, 2 * HEAD_DIM),
        batch=batch, seq=seq, tq=512, rc=128, lambda_init=lambda_init)
    ob = _swa_attention(proj, sinks[l], swa_tbl, batch=batch, seq=seq, tq=512)
    x1, h2 = _outproj(x2, oa, ob, wo_a, wo_b, ffn_norm_w[l].reshape(1, d), tm=512)
    out = _ffn(h2, x1, w_gate[l].astype(BF16), w_up[l].astype(BF16), w_down[l].astype(BF16),
               final_norm_w.reshape(1, d), tm=512, tf=512)
    return out.reshape(batch, seq, d)
```

```python
import functools
import math

import jax
import jax.numpy as jnp
import numpy as np
from jax import lax
from jax.experimental import pallas as pl
from jax.experimental.pallas import tpu as pltpu

D_MODEL = 2048
HEAD_DIM = 64
DIFF_HEADS = 8
DIFF_WIDTH = 1024
SWA_Q_HEADS = 16
SWA_KV_HEADS = 4
SWA_GROUP = 4
SWA_WIDTH = 1024
WINDOW = 128
D_FF = 5632
IN_COLS = 4608
RMS_EPS = 1e-5
LOG2E = math.log2(math.e)
Q_SCALE = HEAD_DIM ** -0.5 * LOG2E
NEG_INF = float("-inf")

V7X_VMEM_BYTES = 64 * 1024 * 1024
VMEM_LIMIT_BYTES = 56 * 1024 * 1024

F32 = jnp.float32
BF16 = jnp.bfloat16


def _alibi_slopes(n_heads):
    return np.array([2.0 ** (-8.0 * (h + 1) / n_heads) for h in range(n_heads)], dtype=np.float64)


def _rms_scale(x):
    return x * lax.rsqrt(jnp.mean(x * x, axis=-1, keepdims=True) + RMS_EPS)


def _norm_inproj_kernel(x_ref, nw_ref, cs_ref, w_ref, o_ref, h_ref, *, tn):
    h_ref[...] = (_rms_scale(x_ref[...]) * nw_ref[...]).astype(BF16)
    for j in range(w_ref.shape[1] // tn):
        cols = slice(j * tn, (j + 1) * tn)
        acc = jnp.dot(h_ref[...], w_ref[:, cols], preferred_element_type=F32)
        o_ref[:, cols] = (acc * cs_ref[:, cols]).astype(o_ref.dtype)


def _norm_inproj(x2, norm_w, col_scale, w_bf16, *, tm, tn):
    tokens, d = x2.shape
    n = w_bf16.shape[1]
    resident = dict(pipeline_mode=pl.Buffered(1))
    return pl.pallas_call(
        functools.partial(_norm_inproj_kernel, tn=tn),
        out_shape=jax.ShapeDtypeStruct((tokens, n), BF16),
        grid=(tokens // tm,),
        in_specs=[
            pl.BlockSpec((tm, d), lambda i: (i, 0)),
            pl.BlockSpec((1, d), lambda i: (0, 0), **resident),
            pl.BlockSpec((1, n), lambda i: (0, 0), **resident),
            pl.BlockSpec((d, n), lambda i: (0, 0), **resident),
        ],
        out_specs=pl.BlockSpec((tm, n), lambda i: (i, 0)),
        scratch_shapes=[pltpu.VMEM((tm, d), BF16)],
        compiler_params=pltpu.CompilerParams(
            dimension_semantics=("parallel",),
            vmem_limit_bytes=VMEM_LIMIT_BYTES,
        ),
        name="norm_inproj",
    )(x2, norm_w, col_scale, w_bf16)


def _diff_attn_kernel(slope_ref, lq1_ref, lk1_ref, lq2_ref, lk2_ref, q_ref, k_ref, v_ref, sw_ref,
                      o_ref, qz_ref, m_ref, acc_ref, *, tq, rc, lambda_init):
    head = pl.program_id(1)
    qi = pl.program_id(2)
    slope2 = slope_ref[head]
    tk = tq
    lanes = 2 * HEAD_DIM
    chunks_per_map = tq // rc
    nt = (((1,), (1,)), ((), ()))

    q = q_ref[...]
    lane = lax.broadcasted_iota(jnp.int32, q.shape, 1)
    zero = jnp.zeros_like(q)
    qz_ref[:tq, :] = jnp.where(lane < HEAD_DIM, q, zero)
    qz_ref[tq:, :] = jnp.where(lane >= HEAD_DIM, q, zero)
    m_ref[...] = jnp.full(m_ref.shape, NEG_INF, F32)
    acc_ref[...] = jnp.zeros(acc_ref.shape, F32)

    kcol = lax.broadcasted_iota(jnp.int32, (1, tk), 1)
    ones = jnp.ones((tk, lanes), BF16)
    tri = lax.broadcasted_iota(jnp.int32, (rc, rc), 1) <= lax.broadcasted_iota(jnp.int32, (rc, rc), 0)

    def chunk_update(c, s, vaug):
        r0 = c * rc
        m_prev = m_ref[r0:r0 + rc, :]
        m_new = jnp.maximum(m_prev, jnp.max(s, axis=1, keepdims=True))
        alpha = jnp.exp2(m_prev - m_new)
        p = jnp.exp2(s - jnp.concatenate([m_new] * (s.shape[1] // lanes), axis=1))
        pv = jnp.dot(p.astype(BF16), vaug, preferred_element_type=F32)
        acc_ref[r0:r0 + rc, :] = jnp.concatenate([alpha, alpha], axis=1) * acc_ref[r0:r0 + rc, :] + pv
        m_ref[r0:r0 + rc, :] = m_new

    def full_step(j, carry):
        k0 = pl.multiple_of(j * tk, tk)
        k = k_ref[pl.ds(k0, tk), :]
        vaug = jnp.concatenate([v_ref[pl.ds(k0, tk), :], ones], axis=1)
        bias = slope2 * (kcol + (j - qi) * tk).astype(F32)
        for c in range(2 * chunks_per_map):
            s = lax.dot_general(qz_ref[c * rc:(c + 1) * rc, :], k, nt, preferred_element_type=F32) + bias
            chunk_update(c, s, vaug)
        return carry

    lax.fori_loop(0, qi, full_step, 0)

    k0 = pl.multiple_of(qi * tk, tk)
    bias = slope2 * kcol.astype(F32)
    for c in range(2 * chunks_per_map):
        a = (c % chunks_per_map) * rc
        ncol = a + rc
        k = k_ref[pl.ds(k0, ncol), :]
        vaug = jnp.concatenate([v_ref[pl.ds(k0, ncol), :], ones[:ncol]], axis=1)
        s = lax.dot_general(qz_ref[c * rc:(c + 1) * rc, :], k, nt, preferred_element_type=F32) + bias[:, :ncol]
        s_diag = jnp.where(tri, s[:, a:], NEG_INF)
        s = jnp.concatenate([s[:, :a], s_diag], axis=1) if a else s_diag
        chunk_update(c, s, vaug)

    lam = (jnp.exp(jnp.sum(lq1_ref[...] * lk1_ref[...], axis=-1, keepdims=True))
           - jnp.exp(jnp.sum(lq2_ref[...] * lk2_ref[...], axis=-1, keepdims=True))
           + lambda_init)
    for c in range(chunks_per_map):
        r1 = c * rc
        r2 = tq + c * rc
        o1 = acc_ref[r1:r1 + rc, :lanes] / acc_ref[r1:r1 + rc, lanes:]
        o2 = acc_ref[r2:r2 + rc, :lanes] / acc_ref[r2:r2 + rc, lanes:]
        o = o1 - lam * o2
        o = _rms_scale(o) * sw_ref[...] * (1.0 - lambda_init)
        o_ref[r1:r1 + rc, :] = o.astype(o_ref.dtype)


def _diff_attention(proj, slopes2, lq1, lk1, lq2, lk2, subln_w, *, batch, seq, tq, rc, lambda_init):
    nq = seq // tq
    lane_blk = 2 * HEAD_DIM
    k_blk0 = DIFF_WIDTH // lane_blk
    v_blk0 = 2 * DIFF_WIDTH // lane_blk
    vec = lambda: pl.BlockSpec((1, HEAD_DIM), lambda b, h, i: (0, 0))
    return pl.pallas_call(
        functools.partial(_diff_attn_kernel, tq=tq, rc=rc, lambda_init=lambda_init),
        out_shape=jax.ShapeDtypeStruct((batch * seq, DIFF_WIDTH), BF16),
        grid=(batch, DIFF_HEADS, nq),
        in_specs=[
            pl.BlockSpec(memory_space=pltpu.SMEM),
            vec(), vec(), vec(), vec(),
            pl.BlockSpec((tq, lane_blk), lambda b, h, i: (b * nq + i, h)),
            pl.BlockSpec((seq, lane_blk), lambda b, h, i: (b, k_blk0 + h)),
            pl.BlockSpec((seq, lane_blk), lambda b, h, i: (b, v_blk0 + h)),
            pl.BlockSpec((1, lane_blk), lambda b, h, i: (0, 0)),
        ],
        out_specs=pl.BlockSpec((tq, lane_blk), lambda b, h, i: (b * nq + i, h)),
        scratch_shapes=[
            pltpu.VMEM((2 * tq, lane_blk), BF16),
            pltpu.VMEM((2 * tq, lane_blk), F32),
            pltpu.VMEM((2 * tq, 2 * lane_blk), F32),
        ],
        compiler_params=pltpu.CompilerParams(
            dimension_semantics=("parallel", "parallel", "arbitrary"),
            vmem_limit_bytes=VMEM_LIMIT_BYTES,
        ),
        name="diff_attention",
    )(slopes2, lq1, lk1, lq2, lk2, proj, proj, proj, subln_w)


def _swa_tables():
    slopes = _alibi_slopes(SWA_Q_HEADS)
    i = np.arange(WINDOW)[:, None]
    j = np.arange(2 * WINDOW)[None, :]
    dist = WINDOW + i - j
    valid = (dist >= 0) & (dist < WINDOW)
    tbl = np.empty((SWA_KV_HEADS // 2, SWA_GROUP * 2 * WINDOW, 2 * WINDOW), np.float32)
    for pair in range(SWA_KV_HEADS // 2):
        for g in range(SWA_GROUP):
            for e in range(2):
                head = (2 * pair + e) * SWA_GROUP + g
                r0 = (g * 2 + e) * WINDOW
                tbl[pair, r0:r0 + WINDOW] = np.where(valid, -slopes[head] * LOG2E * dist, -np.inf)
    return tbl


def _swa_kernel(sink_ref, tbl_ref, q_ref, k_ref, v_ref, o_ref, *, tq):
    qi = pl.program_id(1)
    n_sub = tq // WINDOW
    lane_q = lax.broadcasted_iota(jnp.int32, (WINDOW, 2 * HEAD_DIM), 1)
    lane_kv = lax.broadcasted_iota(jnp.int32, (2 * WINDOW, 2 * HEAD_DIM), 1)
    kcol = lax.broadcasted_iota(jnp.int32, (1, 2 * WINDOW), 1)
    for sub in range(n_sub):
        r0 = sub * WINDOW
        blk = qi * n_sub + sub
        cur0 = pl.multiple_of(blk * WINDOW, WINDOW)
        prev0 = pl.multiple_of(jnp.maximum(blk - 1, 0) * WINDOW, WINDOW)
        for pair in range(SWA_KV_HEADS // 2):
            c0 = pair * 2 * HEAD_DIM
            k2 = jnp.concatenate([k_ref[pl.ds(prev0, WINDOW), c0:c0 + 2 * HEAD_DIM],
                                  k_ref[pl.ds(cur0, WINDOW), c0:c0 + 2 * HEAD_DIM]], axis=0)
            v2 = jnp.concatenate([v_ref[pl.ds(prev0, WINDOW), c0:c0 + 2 * HEAD_DIM],
                                  v_ref[pl.ds(cur0, WINDOW), c0:c0 + 2 * HEAD_DIM]], axis=0)
            zkv = jnp.zeros_like(v2)
            v_even = jnp.where(lane_kv < HEAD_DIM, v2, zkv)
            v_odd = jnp.where(lane_kv >= HEAD_DIM, v2, zkv)
            rows = []
            sink_rows = []
            for g in range(SWA_GROUP):
                t0 = (pair * SWA_GROUP + g) * 2 * HEAD_DIM
                qt = q_ref[r0:r0 + WINDOW, t0:t0 + 2 * HEAD_DIM]
                zq = jnp.zeros_like(qt)
                rows.append(jnp.where(lane_q < HEAD_DIM, qt, zq))
                rows.append(jnp.where(lane_q >= HEAD_DIM, qt, zq))
                for e in range(2):
                    sink_rows.append(jnp.full((WINDOW, 1), sink_ref[(2 * pair + e) * SWA_GROUP + g] * LOG2E, F32))
            q8 = jnp.concatenate(rows, axis=0)
            sink = jnp.concatenate(sink_rows, axis=0)
            s = lax.dot_general(q8, k2, (((1,), (1,)), ((), ())), preferred_element_type=F32)
            s = s + tbl_ref[pair]
            if sub == 0:
                s = s + jnp.where((kcol < WINDOW) & (blk == 0), NEG_INF, 0.0)
            m = jnp.maximum(jnp.max(s, axis=1, keepdims=True), sink)
            p = jnp.exp2(s - m)
            l = jnp.sum(p, axis=1, keepdims=True) + jnp.exp2(sink - m)
            pb = p.astype(BF16)
            inv = 1.0 / l
            for g in range(SWA_GROUP):
                ra = (g * 2) * WINDOW
                rb = (g * 2 + 1) * WINDOW
                oe = jnp.dot(pb[ra:ra + WINDOW], v_even, preferred_element_type=F32)
                oo = jnp.dot(pb[rb:rb + WINDOW], v_odd, preferred_element_type=F32)
                o = oe * inv[ra:ra + WINDOW] + oo * inv[rb:rb + WINDOW]
                t0 = (pair * SWA_GROUP + g) * 2 * HEAD_DIM
                o_ref[r0:r0 + WINDOW, t0:t0 + 2 * HEAD_DIM] = o.astype(o_ref.dtype)


def _swa_attention(proj, sinks, tbl, *, batch, seq, tq):
    nq = seq // tq
    q_blk = (3 * DIFF_WIDTH) // SWA_WIDTH
    kv_w = SWA_KV_HEADS * HEAD_DIM
    k_blk = (3 * DIFF_WIDTH + SWA_WIDTH) // kv_w
    return pl.pallas_call(
        functools.partial(_swa_kernel, tq=tq),
        out_shape=jax.ShapeDtypeStruct((batch * seq, SWA_WIDTH), BF16),
        grid=(batch, nq),
        in_specs=[
            pl.BlockSpec(memory_space=pltpu.SMEM),
            pl.BlockSpec(tbl.shape, lambda b, i: (0, 0, 0)),
            pl.BlockSpec((tq, SWA_WIDTH), lambda b, i: (b * nq + i, q_blk)),
            pl.BlockSpec((seq, kv_w), lambda b, i: (b, k_blk)),
            pl.BlockSpec((seq, kv_w), lambda b, i: (b, k_blk + 1)),
        ],
        out_specs=pl.BlockSpec((tq, SWA_WIDTH), lambda b, i: (b * nq + i, 0)),
        compiler_params=pltpu.CompilerParams(
            dimension_semantics=("parallel", "arbitrary"),
            vmem_limit_bytes=VMEM_LIMIT_BYTES,
        ),
        name="swa_attention",
    )(sinks, tbl, proj, proj, proj)


def _outproj_kernel(x_ref, oa_ref, ob_ref, wa_ref, wb_ref, nw_ref, x1_ref, h_ref):
    y = jnp.dot(oa_ref[...], wa_ref[...], preferred_element_type=F32)
    y = y + jnp.dot(ob_ref[...], wb_ref[...], preferred_element_type=F32)
    x1 = x_ref[...] + y
    x1_ref[...] = x1
    h_ref[...] = (_rms_scale(x1) * nw_ref[...]).astype(h_ref.dtype)


def _outproj(x2, oa, ob, wa, wb, norm_w, *, tm):
    tokens, d = x2.shape
    return pl.pallas_call(
        _outproj_kernel,
        out_shape=(jax.ShapeDtypeStruct((tokens, d), F32), jax.ShapeDtypeStruct((tokens, d), BF16)),
        grid=(tokens // tm,),
        in_specs=[
            pl.BlockSpec((tm, d), lambda i: (i, 0)),
            pl.BlockSpec((tm, DIFF_WIDTH), lambda i: (i, 0)),
            pl.BlockSpec((tm, SWA_WIDTH), lambda i: (i, 0)),
            pl.BlockSpec((DIFF_WIDTH, d), lambda i: (0, 0)),
            pl.BlockSpec((SWA_WIDTH, d), lambda i: (0, 0)),
            pl.BlockSpec((1, d), lambda i: (0, 0)),
        ],
        out_specs=(pl.BlockSpec((tm, d), lambda i: (i, 0)), pl.BlockSpec((tm, d), lambda i: (i, 0))),
        compiler_params=pltpu.CompilerParams(
            dimension_semantics=("parallel",),
            vmem_limit_bytes=VMEM_LIMIT_BYTES,
        ),
        name="outproj_residual_norm",
    )(x2, oa, ob, wa, wb, norm_w)


def _ffn_kernel(h_ref, x1_ref, wg_ref, wu_ref, wd_ref, nw_ref, o_ref):
    f = pl.program_id(1)

    @pl.when(f == 0)
    def _():
        o_ref[...] = x1_ref[...]

    h = h_ref[...]
    g = jnp.dot(h, wg_ref[...], preferred_element_type=F32)
    u = jnp.dot(h, wu_ref[...], preferred_element_type=F32)
    a = (g * (1.0 / (1.0 + jnp.exp(-g))) * u).astype(BF16)
    o_ref[...] += jnp.dot(a, wd_ref[...], preferred_element_type=F32)

    @pl.when(f == pl.num_programs(1) - 1)
    def _():
        o_ref[...] = _rms_scale(o_ref[...]) * nw_ref[...]


def _ffn(h2, x1, wg, wu, wd, norm_w, *, tm, tf):
    tokens, d = x1.shape
    dff = wg.shape[1]
    return pl.pallas_call(
        _ffn_kernel,
        out_shape=jax.ShapeDtypeStruct((tokens, d), F32),
        grid=(tokens // tm, dff // tf),
        in_specs=[
            pl.BlockSpec((tm, d), lambda i, f: (i, 0)),
            pl.BlockSpec((tm, d), lambda i, f: (i, 0)),
            pl.BlockSpec((d, tf), lambda i, f: (0, f)),
            pl.BlockSpec((d, tf), lambda i, f: (0, f)),
            pl.BlockSpec((tf, d), lambda i, f: (f, 0)),
            pl.BlockSpec((1, d), lambda i, f: (0, 0)),
        ],
        out_specs=pl.BlockSpec((tm, d), lambda i, f: (i, 0)),
        compiler_params=pltpu.CompilerParams(
            dimension_semantics=("parallel", "arbitrary"),
            vmem_limit_bytes=VMEM_LIMIT_BYTES,
        ),
        name="swiglu_ffn_final_norm",
    )(h2, x1, wg, wu, wd, norm_w)


def kernel(x, attn_norm_w, w_in, lambda_q1, lambda_k1, lambda_q2, lambda_k2, subln_w, sinks, w_out,
           ffn_norm_w, w_gate, w_up, w_down, final_norm_w):
    batch, seq, d = x.shape
    depth = w_in.shape[0]
    assert (d, w_in.shape[2], w_gate.shape[2]) == (D_MODEL, IN_COLS, D_FF)
    tokens = batch * seq
    x2 = x.reshape(tokens, d)

    q_cols = np.ones((1, IN_COLS), np.float32)
    q_cols[:, :DIFF_WIDTH] = Q_SCALE
    q_cols[:, 3 * DIFF_WIDTH:3 * DIFF_WIDTH + SWA_WIDTH] = Q_SCALE
    col_scale = jnp.asarray(q_cols)
    diff_slopes2 = jnp.asarray((_alibi_slopes(DIFF_HEADS) * LOG2E).astype(np.float32))
    swa_tbl = jnp.asarray(_swa_tables())
    n_pair = SWA_KV_HEADS // 2

    assert depth == 1
    l = 0
    lambda_init = 0.8 - 0.6 * math.exp(-0.3 * l)
    wq = w_in[l][:, 3 * DIFF_WIDTH:3 * DIFF_WIDTH + SWA_WIDTH]
    wq = wq.reshape(d, n_pair, 2, SWA_GROUP, HEAD_DIM).transpose(0, 1, 3, 2, 4).reshape(d, SWA_WIDTH)
    w_in_b = jnp.concatenate(
        [w_in[l][:, :3 * DIFF_WIDTH], wq, w_in[l][:, 3 * DIFF_WIDTH + SWA_WIDTH:]], axis=1).astype(BF16)
    wo_a = w_out[l][:DIFF_WIDTH].astype(BF16)
    wo_b = w_out[l][DIFF_WIDTH:].reshape(n_pair, 2, SWA_GROUP, HEAD_DIM, d).transpose(0, 2, 1, 3, 4)
    wo_b = wo_b.reshape(SWA_WIDTH, d).astype(BF16)

    proj = _norm_inproj(x2, attn_norm_w[l].reshape(1, d), col_scale, w_in_b, tm=512, tn=512)
    oa = _diff_attention(
        proj, diff_slopes2,
        lambda_q1[l].reshape(1, HEAD_DIM), lambda_k1[l].reshape(1, HEAD_DIM),
        lambda_q2[l].reshape(1, HEAD_DIM), lambda_k2[l].reshape(1, HEAD_DIM),
        subln_w[l].reshape(1, 2 * HEAD_DIM),
        batch=batch, seq=seq, tq=512, rc=128, lambda_init=lambda_init)
    ob = _swa_attention(proj, sinks[l], swa_tbl, batch=batch, seq=seq, tq=512)
    x1, h2 = _outproj(x2, oa, ob, wo_a, wo_b, ffn_norm_w[l].reshape(1, d), tm=512)
    out = _ffn(h2, x1, w_gate[l].astype(BF16), w_up[l].astype(BF16), w_down[l].astype(BF16),
               final_norm_w.reshape(1, d), tm=512, tf=512)
    return out.reshape(batch, seq, d)
```

```python
import functools
import math

import jax
import jax.numpy as jnp
import numpy as np
from jax import lax
from jax.experimental import pallas as pl
from jax.experimental.pallas import tpu as pltpu

D_MODEL = 2048
HEAD_DIM = 64
DIFF_HEADS = 8
DIFF_WIDTH = 1024
SWA_Q_HEADS = 16
SWA_KV_HEADS = 4
SWA_GROUP = 4
SWA_WIDTH = 1024
WINDOW = 128
D_FF = 5632
IN_COLS = 4608
RMS_EPS = 1e-5
LOG2E = math.log2(math.e)
Q_SCALE = HEAD_DIM ** -0.5 * LOG2E
NEG_INF = float("-inf")

V7X_VMEM_BYTES = 64 * 1024 * 1024
VMEM_LIMIT_BYTES = 56 * 1024 * 1024

F32 = jnp.float32
BF16 = jnp.bfloat16


def _alibi_slopes(n_heads):
    return np.array([2.0 ** (-8.0 * (h + 1) / n_heads) for h in range(n_heads)], dtype=np.float64)


def _rms_scale(x):
    return x * lax.rsqrt(jnp.mean(x * x, axis=-1, keepdims=True) + RMS_EPS)


def _norm_inproj_kernel(x_ref, nw_ref, cs_ref, w_ref, o_ref, h_ref, *, tn):
    h_ref[...] = (_rms_scale(x_ref[...]) * nw_ref[...]).astype(BF16)
    for j in range(w_ref.shape[1] // tn):
        cols = slice(j * tn, (j + 1) * tn)
        acc = jnp.dot(h_ref[...], w_ref[:, cols], preferred_element_type=F32)
        o_ref[:, cols] = (acc * cs_ref[:, cols]).astype(o_ref.dtype)


def _norm_inproj(x2, norm_w, col_scale, w_bf16, *, tm, tn):
    tokens, d = x2.shape
    n = w_bf16.shape[1]
    resident = dict(pipeline_mode=pl.Buffered(1))
    return pl.pallas_call(
        functools.partial(_norm_inproj_kernel, tn=tn),
        out_shape=jax.ShapeDtypeStruct((tokens, n), BF16),
        grid=(tokens // tm,),
        in_specs=[
            pl.BlockSpec((tm, d), lambda i: (i, 0)),
            pl.BlockSpec((1, d), lambda i: (0, 0), **resident),
            pl.BlockSpec((1, n), lambda i: (0, 0), **resident),
            pl.BlockSpec((d, n), lambda i: (0, 0), **resident),
        ],
        out_specs=pl.BlockSpec((tm, n), lambda i: (i, 0)),
        scratch_shapes=[pltpu.VMEM((tm, d), BF16)],
        compiler_params=pltpu.CompilerParams(
            dimension_semantics=("parallel",),
            vmem_limit_bytes=VMEM_LIMIT_BYTES,
        ),
        name="norm_inproj",
    )(x2, norm_w, col_scale, w_bf16)


def _diff_attn_kernel(slope_ref, lq1_ref, lk1_ref, lq2_ref, lk2_ref, q_ref, k_ref, v_ref, sw_ref,
                      o_ref, m_ref, acc_ref, *, tk, rc, lambda_init):
    seq = q_ref.shape[0]
    slope2 = slope_ref[pl.program_id(1)]
    lanes = 2 * HEAD_DIM
    n_blk = seq // tk
    nt = (((1,), (1,)), ((), ()))

    first_map = lax.broadcasted_iota(jnp.int32, (rc, lanes), 1) < HEAD_DIM
    kcol = lax.broadcasted_iota(jnp.int32, (1, tk), 1).astype(F32)
    ones = jnp.ones((tk, lanes), BF16)
    tri = lax.broadcasted_iota(jnp.int32, (rc, rc), 1) <= lax.broadcasted_iota(jnp.int32, (rc, rc), 0)
    lam = (jnp.exp(jnp.sum(lq1_ref[...] * lk1_ref[...], axis=-1, keepdims=True))
           - jnp.exp(jnp.sum(lq2_ref[...] * lk2_ref[...], axis=-1, keepdims=True))
           + lambda_init)

    for j in range(n_blk):
        k0 = j * tk
        for i in range(j, n_blk):
            bias = slope2 * (kcol + float(k0 - i * tk))
            for sub in range(tk // rc):
                r0 = i * tk + sub * rc
                ncol = sub * rc + rc if i == j else tk
                k = k_ref[k0:k0 + ncol, :]
                vaug = jnp.concatenate([v_ref[k0:k0 + ncol, :], ones[:ncol]], axis=1)
                qc = q_ref[r0:r0 + rc, :]
                zero = jnp.zeros_like(qc)
                for mp in range(2):
                    qz = jnp.where(first_map if mp == 0 else ~first_map, qc, zero)
                    s = lax.dot_general(qz, k, nt, preferred_element_type=F32) + bias[:, :ncol]
                    if i == j:
                        s_diag = jnp.where(tri, s[:, ncol - rc:], NEG_INF)
                        s = jnp.concatenate([s[:, :ncol - rc], s_diag], axis=1) if ncol > rc else s_diag
                    rows = slice(mp * seq + r0, mp * seq + r0 + rc)
                    m_cur = jnp.max(s, axis=1, keepdims=True)
                    if j == 0:
                        m_new = jnp.broadcast_to(m_cur, (rc, lanes))
                    else:
                        m_prev = m_ref[rows, :]
                        m_new = jnp.maximum(m_prev, m_cur)
                    p = jnp.exp2(s - jnp.concatenate([m_new] * (ncol // lanes), axis=1))
                    pv = jnp.dot(p.astype(BF16), vaug, preferred_element_type=F32)
                    if j > 0:
                        alpha = jnp.exp2(m_prev - m_new)
                        pv = jnp.concatenate([alpha, alpha], axis=1) * acc_ref[rows, :] + pv
                    if i > j:
                        m_ref[rows, :] = m_new
                        acc_ref[rows, :] = pv
                    elif mp == 0:
                        o1 = pv[:, :lanes] / pv[:, lanes:]
                    else:
                        o = o1 - lam * (pv[:, :lanes] / pv[:, lanes:])
                        o = _rms_scale(o) * sw_ref[...] * (1.0 - lambda_init)
                        o_ref[r0:r0 + rc, :] = o.astype(o_ref.dtype)


def _diff_attention(proj, slopes2, lq1, lk1, lq2, lk2, subln_w, *, batch, seq, tk, rc, lambda_init):
    lane_blk = 2 * HEAD_DIM
    k_blk0 = DIFF_WIDTH // lane_blk
    v_blk0 = 2 * DIFF_WIDTH // lane_blk
    vec = lambda: pl.BlockSpec((1, HEAD_DIM), lambda b, h: (0, 0))
    return pl.pallas_call(
        functools.partial(_diff_attn_kernel, tk=tk, rc=rc, lambda_init=lambda_init),
        out_shape=jax.ShapeDtypeStruct((batch * seq, DIFF_WIDTH), BF16),
        grid=(batch, DIFF_HEADS),
        in_specs=[
            pl.BlockSpec(memory_space=pltpu.SMEM),
            vec(), vec(), vec(), vec(),
            pl.BlockSpec((seq, lane_blk), lambda b, h: (b, h)),
            pl.BlockSpec((seq, lane_blk), lambda b, h: (b, k_blk0 + h)),
            pl.BlockSpec((seq, lane_blk), lambda b, h: (b, v_blk0 + h)),
            pl.BlockSpec((1, lane_blk), lambda b, h: (0, 0)),
        ],
        out_specs=pl.BlockSpec((seq, lane_blk), lambda b, h: (b, h)),
        scratch_shapes=[
            pltpu.VMEM((2 * seq, lane_blk), F32),
            pltpu.VMEM((2 * seq, 2 * lane_blk), F32),
        ],
        compiler_params=pltpu.CompilerParams(
            dimension_semantics=("parallel", "parallel"),
            vmem_limit_bytes=VMEM_LIMIT_BYTES,
        ),
        name="diff_attention",
    )(slopes2, lq1, lk1, lq2, lk2, proj, proj, proj, subln_w)


def _swa_tables():
    slopes = _alibi_slopes(SWA_Q_HEADS)
    i = np.arange(WINDOW)[:, None]
    j = np.arange(2 * WINDOW)[None, :]
    dist = WINDOW + i - j
    valid = (dist >= 0) & (dist < WINDOW)
    tbl = np.empty((SWA_KV_HEADS // 2, SWA_GROUP * 2 * WINDOW, 2 * WINDOW), np.float32)
    for pair in range(SWA_KV_HEADS // 2):
        for g in range(SWA_GROUP):
            for e in range(2):
                head = (2 * pair + e) * SWA_GROUP + g
                r0 = (g * 2 + e) * WINDOW
                tbl[pair, r0:r0 + WINDOW] = np.where(valid, -slopes[head] * LOG2E * dist, -np.inf)
    return tbl


def _swa_kernel(sink_ref, tbl_ref, q_ref, k_ref, v_ref, o_ref, *, tq):
    qi = pl.program_id(1)
    n_sub = tq // WINDOW
    lanes = 2 * HEAD_DIM
    lane_q = lax.broadcasted_iota(jnp.int32, (WINDOW, lanes), 1)
    lane_kv = lax.broadcasted_iota(jnp.int32, (2 * WINDOW, lanes), 1)
    ones_kv = jnp.ones((2 * WINDOW, lanes), BF16)
    kcol = lax.broadcasted_iota(jnp.int32, (1, 2 * WINDOW), 1)
    for sub in range(n_sub):
        r0 = sub * WINDOW
        blk = qi * n_sub + sub
        cur0 = pl.multiple_of(blk * WINDOW, WINDOW)
        prev0 = pl.multiple_of(jnp.maximum(blk - 1, 0) * WINDOW, WINDOW)
        for pair in range(SWA_KV_HEADS // 2):
            c0 = pair * 2 * HEAD_DIM
            k2 = jnp.concatenate([k_ref[pl.ds(prev0, WINDOW), c0:c0 + 2 * HEAD_DIM],
                                  k_ref[pl.ds(cur0, WINDOW), c0:c0 + 2 * HEAD_DIM]], axis=0)
            v2 = jnp.concatenate([v_ref[pl.ds(prev0, WINDOW), c0:c0 + 2 * HEAD_DIM],
                                  v_ref[pl.ds(cur0, WINDOW), c0:c0 + 2 * HEAD_DIM]], axis=0)
            zkv = jnp.zeros_like(v2)
            v_aug = [jnp.concatenate([jnp.where(lane_kv < HEAD_DIM, v2, zkv), ones_kv], axis=1),
                     jnp.concatenate([jnp.where(lane_kv >= HEAD_DIM, v2, zkv), ones_kv], axis=1)]
            rows = []
            for g in range(SWA_GROUP):
                t0 = (pair * SWA_GROUP + g) * 2 * HEAD_DIM
                qt = q_ref[r0:r0 + WINDOW, t0:t0 + 2 * HEAD_DIM]
                zq = jnp.zeros_like(qt)
                rows.append(jnp.where(lane_q < HEAD_DIM, qt, zq))
                rows.append(jnp.where(lane_q >= HEAD_DIM, qt, zq))
            q8 = jnp.concatenate(rows, axis=0)
            s = lax.dot_general(q8, k2, (((1,), (1,)), ((), ())), preferred_element_type=F32)
            for g in range(SWA_GROUP):
                o = None
                for e in range(2):
                    rb = (g * 2 + e) * WINDOW
                    sb = s[rb:rb + WINDOW] + tbl_ref[pair, rb:rb + WINDOW, :]
                    if sub == 0:
                        sb = sb + jnp.where((kcol < WINDOW) & (blk == 0), NEG_INF, 0.0)
                    sink = jnp.full((WINDOW, lanes), sink_ref[(2 * pair + e) * SWA_GROUP + g] * LOG2E, F32)
                    m = jnp.maximum(jnp.max(sb, axis=1, keepdims=True), sink)
                    p = jnp.exp2(sb - jnp.concatenate([m, m], axis=1))
                    pv = jnp.dot(p.astype(BF16), v_aug[e], preferred_element_type=F32)
                    part = pv[:, :lanes] / (pv[:, lanes:] + jnp.exp2(sink - m))
                    o = part if o is None else o + part
                t0 = (pair * SWA_GROUP + g) * 2 * HEAD_DIM
                o_ref[r0:r0 + WINDOW, t0:t0 + 2 * HEAD_DIM] = o.astype(o_ref.dtype)


def _swa_attention(proj, sinks, tbl, *, batch, seq, tq):
    nq = seq // tq
    q_blk = (3 * DIFF_WIDTH) // SWA_WIDTH
    kv_w = SWA_KV_HEADS * HEAD_DIM
    k_blk = (3 * DIFF_WIDTH + SWA_WIDTH) // kv_w
    return pl.pallas_call(
        functools.partial(_swa_kernel, tq=tq),
        out_shape=jax.ShapeDtypeStruct((batch * seq, SWA_WIDTH), BF16),
        grid=(batch, nq),
        in_specs=[
            pl.BlockSpec(memory_space=pltpu.SMEM),
            pl.BlockSpec(tbl.shape, lambda b, i: (0, 0, 0)),
            pl.BlockSpec((tq, SWA_WIDTH), lambda b, i: (b * nq + i, q_blk)),
            pl.BlockSpec((seq, kv_w), lambda b, i: (b, k_blk)),
            pl.BlockSpec((seq, kv_w), lambda b, i: (b, k_blk + 1)),
        ],
        out_specs=pl.BlockSpec((tq, SWA_WIDTH), lambda b, i: (b * nq + i, 0)),
        compiler_params=pltpu.CompilerParams(
            dimension_semantics=("parallel", "arbitrary"),
            vmem_limit_bytes=VMEM_LIMIT_BYTES,
        ),
        name="swa_attention",
    )(sinks, tbl, proj, proj, proj)


def _outproj_kernel(x_ref, oa_ref, ob_ref, wa_ref, wb_ref, nw_ref, x1_ref, h_ref):
    y = jnp.dot(oa_ref[...], wa_ref[...], preferred_element_type=F32)
    y = y + jnp.dot(ob_ref[...], wb_ref[...], preferred_element_type=F32)
    x1 = x_ref[...] + y
    x1_ref[...] = x1
    h_ref[...] = (_rms_scale(x1) * nw_ref[...]).astype(h_ref.dtype)


def _outproj(x2, oa, ob, wa, wb, norm_w, *, tm):
    tokens, d = x2.shape
    return pl.pallas_call(
        _outproj_kernel,
        out_shape=(jax.ShapeDtypeStruct((tokens, d), F32), jax.ShapeDtypeStruct((tokens, d), BF16)),
        grid=(tokens // tm,),
        in_specs=[
            pl.BlockSpec((tm, d), lambda i: (i, 0)),
            pl.BlockSpec((tm, DIFF_WIDTH), lambda i: (i, 0)),
            pl.BlockSpec((tm, SWA_WIDTH), lambda i: (i, 0)),
            pl.BlockSpec((DIFF_WIDTH, d), lambda i: (0, 0)),
            pl.BlockSpec((SWA_WIDTH, d), lambda i: (0, 0)),
            pl.BlockSpec((1, d), lambda i: (0, 0)),
        ],
        out_specs=(pl.BlockSpec((tm, d), lambda i: (i, 0)), pl.BlockSpec((tm, d), lambda i: (i, 0))),
        compiler_params=pltpu.CompilerParams(
            dimension_semantics=("parallel",),
            vmem_limit_bytes=VMEM_LIMIT_BYTES,
        ),
        name="outproj_residual_norm",
    )(x2, oa, ob, wa, wb, norm_w)


def _ffn_kernel(h_ref, x1_ref, wg_ref, wu_ref, wd_ref, nw_ref, o_ref):
    f = pl.program_id(1)

    @pl.when(f == 0)
    def _():
        o_ref[...] = x1_ref[...]

    h = h_ref[...]
    g = jnp.dot(h, wg_ref[...], preferred_element_type=F32)
    u = jnp.dot(h, wu_ref[...], preferred_element_type=F32)
    a = (g * (1.0 / (1.0 + jnp.exp(-g))) * u).astype(BF16)
    o_ref[...] += jnp.dot(a, wd_ref[...], preferred_element_type=F32)

    @pl.when(f == pl.num_programs(1) - 1)
    def _():
        o_ref[...] = _rms_scale(o_ref[...]) * nw_ref[...]


def _ffn(h2, x1, wg, wu, wd, norm_w, *, tm, tf):
    tokens, d = x1.shape
    dff = wg.shape[1]
    return pl.pallas_call(
        _ffn_kernel,
        out_shape=jax.ShapeDtypeStruct((tokens, d), F32),
        grid=(tokens // tm, dff // tf),
        in_specs=[
            pl.BlockSpec((tm, d), lambda i, f: (i, 0)),
            pl.BlockSpec((tm, d), lambda i, f: (i, 0)),
            pl.BlockSpec((d, tf), lambda i, f: (0, f)),
            pl.BlockSpec((d, tf), lambda i, f: (0, f)),
            pl.BlockSpec((tf, d), lambda i, f: (f, 0)),
            pl.BlockSpec((1, d), lambda i, f: (0, 0)),
        ],
        out_specs=pl.BlockSpec((tm, d), lambda i, f: (i, 0)),
        compiler_params=pltpu.CompilerParams(
            dimension_semantics=("parallel", "arbitrary"),
            vmem_limit_bytes=VMEM_LIMIT_BYTES,
        ),
        name="swiglu_ffn_final_norm",
    )(h2, x1, wg, wu, wd, norm_w)


def kernel(x, attn_norm_w, w_in, lambda_q1, lambda_k1, lambda_q2, lambda_k2, subln_w, sinks, w_out,
           ffn_norm_w, w_gate, w_up, w_down, final_norm_w):
    batch, seq, d = x.shape
    depth = w_in.shape[0]
    assert (d, w_in.shape[2], w_gate.shape[2]) == (D_MODEL, IN_COLS, D_FF)
    tokens = batch * seq
    x2 = x.reshape(tokens, d)

    q_cols = np.ones((1, IN_COLS), np.float32)
    q_cols[:, :DIFF_WIDTH] = Q_SCALE
    q_cols[:, 3 * DIFF_WIDTH:3 * DIFF_WIDTH + SWA_WIDTH] = Q_SCALE
    col_scale = jnp.asarray(q_cols)
    diff_slopes2 = jnp.asarray((_alibi_slopes(DIFF_HEADS) * LOG2E).astype(np.float32))
    swa_tbl = jnp.asarray(_swa_tables())
    n_pair = SWA_KV_HEADS // 2

    assert depth == 1
    l = 0
    lambda_init = 0.8 - 0.6 * math.exp(-0.3 * l)
    wq = w_in[l][:, 3 * DIFF_WIDTH:3 * DIFF_WIDTH + SWA_WIDTH]
    wq = wq.reshape(d, n_pair, 2, SWA_GROUP, HEAD_DIM).transpose(0, 1, 3, 2, 4).reshape(d, SWA_WIDTH)
    w_in_b = jnp.concatenate(
        [w_in[l][:, :3 * DIFF_WIDTH], wq, w_in[l][:, 3 * DIFF_WIDTH + SWA_WIDTH:]], axis=1).astype(BF16)
    wo_a = w_out[l][:DIFF_WIDTH].astype(BF16)
    wo_b = w_out[l][DIFF_WIDTH:].reshape(n_pair, 2, SWA_GROUP, HEAD_DIM, d).transpose(0, 2, 1, 3, 4)
    wo_b = wo_b.reshape(SWA_WIDTH, d).astype(BF16)

    proj = _norm_inproj(x2, attn_norm_w[l].reshape(1, d), col_scale, w_in_b, tm=512, tn=512)
    oa = _diff_attention(
        proj, diff_slopes2,
        lambda_q1[l].reshape(1, HEAD_DIM), lambda_k1[l].reshape(1, HEAD_DIM),
        lambda_q2[l].reshape(1, HEAD_DIM), lambda_k2[l].reshape(1, HEAD_DIM),
        subln_w[l].reshape(1, 2 * HEAD_DIM),
        batch=batch, seq=seq, tk=512, rc=256, lambda_init=lambda_init)
    ob = _swa_attention(proj, sinks[l], swa_tbl, batch=batch, seq=seq, tq=512)
    x1, h2 = _outproj(x2, oa, ob, wo_a, wo_b, ffn_norm_w[l].reshape(1, d), tm=512)
    out = _ffn(h2, x1, w_gate[l].astype(BF16), w_up[l].astype(BF16), w_down[l].astype(BF16),
               final_norm_w.reshape(1, d), tm=512, tf=512)
    return out.reshape(batch, seq, d)
```

```python
import functools
import math

import jax
import jax.numpy as jnp
import numpy as np
from jax import lax
from jax.experimental import pallas as pl
from jax.experimental.pallas import tpu as pltpu

D_MODEL = 2048
HEAD_DIM = 64
DIFF_HEADS = 8
DIFF_WIDTH = 1024
SWA_Q_HEADS = 16
SWA_KV_HEADS = 4
SWA_GROUP = 4
SWA_WIDTH = 1024
WINDOW = 128
D_FF = 5632
IN_COLS = 4608
RMS_EPS = 1e-5
LOG2E = math.log2(math.e)
Q_SCALE = HEAD_DIM ** -0.5 * LOG2E
NEG_INF = float("-inf")

V7X_VMEM_BYTES = 64 * 1024 * 1024
VMEM_LIMIT_BYTES = 56 * 1024 * 1024

F32 = jnp.float32
BF16 = jnp.bfloat16


def _alibi_slopes(n_heads):
    return np.array([2.0 ** (-8.0 * (h + 1) / n_heads) for h in range(n_heads)], dtype=np.float64)


def _rms_scale(x):
    return x * lax.rsqrt(jnp.mean(x * x, axis=-1, keepdims=True) + RMS_EPS)


def _norm_inproj_kernel(x_ref, nw_ref, cs_ref, w_ref, o_ref, h_ref, *, tn):
    h_ref[...] = (_rms_scale(x_ref[...]) * nw_ref[...]).astype(BF16)
    for j in range(w_ref.shape[1] // tn):
        cols = slice(j * tn, (j + 1) * tn)
        acc = jnp.dot(h_ref[...], w_ref[:, cols], preferred_element_type=F32)
        o_ref[:, cols] = (acc * cs_ref[:, cols]).astype(o_ref.dtype)


def _norm_inproj(x2, norm_w, col_scale, w_bf16, *, tm, tn):
    tokens, d = x2.shape
    n = w_bf16.shape[1]
    resident = dict(pipeline_mode=pl.Buffered(1))
    return pl.pallas_call(
        functools.partial(_norm_inproj_kernel, tn=tn),
        out_shape=jax.ShapeDtypeStruct((tokens, n), BF16),
        grid=(tokens // tm,),
        in_specs=[
            pl.BlockSpec((tm, d), lambda i: (i, 0)),
            pl.BlockSpec((1, d), lambda i: (0, 0), **resident),
            pl.BlockSpec((1, n), lambda i: (0, 0), **resident),
            pl.BlockSpec((d, n), lambda i: (0, 0), **resident),
        ],
        out_specs=pl.BlockSpec((tm, n), lambda i: (i, 0)),
        scratch_shapes=[pltpu.VMEM((tm, d), BF16)],
        compiler_params=pltpu.CompilerParams(
            dimension_semantics=("parallel",),
            vmem_limit_bytes=VMEM_LIMIT_BYTES,
        ),
        name="norm_inproj",
    )(x2, norm_w, col_scale, w_bf16)


def _diff_attn_kernel(slope_ref, lq1_ref, lk1_ref, lq2_ref, lk2_ref, q_ref, k_ref, v_ref, sw_ref,
                      o_ref, m_ref, acc_ref, *, hp, tk, rc, lambda_init):
    seq = q_ref.shape[0]
    lanes = 2 * HEAD_DIM
    n_blk = seq // tk
    nt = (((1,), (1,)), ((), ()))
    slope2 = [slope_ref[pl.program_id(1) * hp + hh] for hh in range(hp)]

    first_map = lax.broadcasted_iota(jnp.int32, (rc, lanes), 1) < HEAD_DIM
    kcol = lax.broadcasted_iota(jnp.int32, (1, tk), 1).astype(F32)
    ones = jnp.ones((tk, lanes), BF16)
    tri = lax.broadcasted_iota(jnp.int32, (rc, rc), 1) <= lax.broadcasted_iota(jnp.int32, (rc, rc), 0)
    lam = (jnp.exp(jnp.sum(lq1_ref[...] * lk1_ref[...], axis=-1, keepdims=True))
           - jnp.exp(jnp.sum(lq2_ref[...] * lk2_ref[...], axis=-1, keepdims=True))
           + lambda_init)

    steps = [(hh, i, j, sub, mp) for hh in range(hp) for j in range(n_blk) for i in range(j, n_blk)
             for sub in range(tk // rc) for mp in range(2)]

    def n_cols(i, j, sub):
        return sub * rc + rc if i == j else tk

    def logits(hh, i, j, sub, mp):
        k0, r0, ncol = j * tk, i * tk + sub * rc, n_cols(i, j, sub)
        hl = slice(hh * lanes, (hh + 1) * lanes)
        qc = q_ref[r0:r0 + rc, hl]
        qz = jnp.where(first_map if mp == 0 else ~first_map, qc, jnp.zeros_like(qc))
        bias = slope2[hh] * (kcol[:, :ncol] + float(k0 - i * tk))
        s = lax.dot_general(qz, k_ref[k0:k0 + ncol, hl], nt, preferred_element_type=F32) + bias
        if i == j:
            s_diag = jnp.where(tri, s[:, ncol - rc:], NEG_INF)
            s = jnp.concatenate([s[:, :ncol - rc], s_diag], axis=1) if ncol > rc else s_diag
        return s

    def update(hh, i, j, sub, mp, s, o1):
        k0, r0, ncol = j * tk, i * tk + sub * rc, n_cols(i, j, sub)
        hl = slice(hh * lanes, (hh + 1) * lanes)
        vaug = jnp.concatenate([v_ref[k0:k0 + ncol, hl], ones[:ncol]], axis=1)
        row0 = (2 * hh + mp) * seq + r0
        rows = slice(row0, row0 + rc)
        m_cur = jnp.max(s, axis=1, keepdims=True)
        if j == 0:
            m_new = jnp.broadcast_to(m_cur, (rc, lanes))
        else:
            m_prev = m_ref[rows, :]
            m_new = jnp.maximum(m_prev, m_cur)
        p = jnp.exp2(s - jnp.concatenate([m_new] * (ncol // lanes), axis=1))
        pv = jnp.dot(p.astype(BF16), vaug, preferred_element_type=F32)
        if j > 0:
            alpha = jnp.exp2(m_prev - m_new)
            pv = jnp.concatenate([alpha, alpha], axis=1) * acc_ref[rows, :] + pv
        if i > j:
            m_ref[rows, :] = m_new
            acc_ref[rows, :] = pv
            return None
        if mp == 0:
            return pv[:, :lanes] / pv[:, lanes:]
        o = o1 - lam * (pv[:, :lanes] / pv[:, lanes:])
        o = _rms_scale(o) * sw_ref[...] * (1.0 - lambda_init)
        o_ref[r0:r0 + rc, hl] = o.astype(o_ref.dtype)
        return None

    ahead = 2
    pending = [logits(*st) for st in steps[:ahead]]
    o1 = None
    for n, step in enumerate(steps):
        s_cur = pending.pop(0)
        if n + ahead < len(steps):
            pending.append(logits(*steps[n + ahead]))
        o1 = update(*step, s_cur, o1)


def _diff_attention(proj, slopes2, lq1, lk1, lq2, lk2, subln_w, *, batch, seq, hp, tk, rc, lambda_init):
    lane_blk = hp * 2 * HEAD_DIM
    k_blk0 = DIFF_WIDTH // lane_blk
    v_blk0 = 2 * DIFF_WIDTH // lane_blk
    vec = lambda: pl.BlockSpec((1, HEAD_DIM), lambda b, h: (0, 0))
    return pl.pallas_call(
        functools.partial(_diff_attn_kernel, hp=hp, tk=tk, rc=rc, lambda_init=lambda_init),
        out_shape=jax.ShapeDtypeStruct((batch * seq, DIFF_WIDTH), BF16),
        grid=(batch, DIFF_HEADS // hp),
        in_specs=[
            pl.BlockSpec(memory_space=pltpu.SMEM),
            vec(), vec(), vec(), vec(),
            pl.BlockSpec((seq, lane_blk), lambda b, h: (b, h)),
            pl.BlockSpec((seq, lane_blk), lambda b, h: (b, k_blk0 + h)),
            pl.BlockSpec((seq, lane_blk), lambda b, h: (b, v_blk0 + h)),
            pl.BlockSpec((1, 2 * HEAD_DIM), lambda b, h: (0, 0)),
        ],
        out_specs=pl.BlockSpec((seq, lane_blk), lambda b, h: (b, h)),
        scratch_shapes=[
            pltpu.VMEM((hp * 2 * seq, 2 * HEAD_DIM), F32),
            pltpu.VMEM((hp * 2 * seq, 4 * HEAD_DIM), F32),
        ],
        compiler_params=pltpu.CompilerParams(
            dimension_semantics=("parallel", "parallel"),
            vmem_limit_bytes=VMEM_LIMIT_BYTES,
        ),
        name="diff_attention",
    )(slopes2, lq1, lk1, lq2, lk2, proj, proj, proj, subln_w)


def _swa_tables():
    slopes = _alibi_slopes(SWA_Q_HEADS)
    i = np.arange(WINDOW)[:, None]
    j = np.arange(2 * WINDOW)[None, :]
    dist = WINDOW + i - j
    valid = (dist >= 0) & (dist < WINDOW)
    tbl = np.empty((SWA_KV_HEADS // 2, SWA_GROUP * 2 * WINDOW, 2 * WINDOW), np.float32)
    for pair in range(SWA_KV_HEADS // 2):
        for g in range(SWA_GROUP):
            for e in range(2):
                head = (2 * pair + e) * SWA_GROUP + g
                r0 = (g * 2 + e) * WINDOW
                tbl[pair, r0:r0 + WINDOW] = np.where(valid, -slopes[head] * LOG2E * dist, -np.inf)
    return tbl


def _swa_kernel(sink_ref, tbl_ref, q_ref, k_ref, v_ref, o_ref, *, tq):
    qi = pl.program_id(1)
    n_sub = tq // WINDOW
    lanes = 2 * HEAD_DIM
    lane_q = lax.broadcasted_iota(jnp.int32, (WINDOW, lanes), 1)
    lane_kv = lax.broadcasted_iota(jnp.int32, (2 * WINDOW, lanes), 1)
    ones_kv = jnp.ones((2 * WINDOW, lanes), BF16)
    kcol = lax.broadcasted_iota(jnp.int32, (1, 2 * WINDOW), 1)
    for sub in range(n_sub):
        r0 = sub * WINDOW
        blk = qi * n_sub + sub
        cur0 = pl.multiple_of(blk * WINDOW, WINDOW)
        prev0 = pl.multiple_of(jnp.maximum(blk - 1, 0) * WINDOW, WINDOW)
        for pair in range(SWA_KV_HEADS // 2):
            c0 = pair * 2 * HEAD_DIM
            k2 = jnp.concatenate([k_ref[pl.ds(prev0, WINDOW), c0:c0 + 2 * HEAD_DIM],
                                  k_ref[pl.ds(cur0, WINDOW), c0:c0 + 2 * HEAD_DIM]], axis=0)
            v2 = jnp.concatenate([v_ref[pl.ds(prev0, WINDOW), c0:c0 + 2 * HEAD_DIM],
                                  v_ref[pl.ds(cur0, WINDOW), c0:c0 + 2 * HEAD_DIM]], axis=0)
            zkv = jnp.zeros_like(v2)
            v_aug = [jnp.concatenate([jnp.where(lane_kv < HEAD_DIM, v2, zkv), ones_kv], axis=1),
                     jnp.concatenate([jnp.where(lane_kv >= HEAD_DIM, v2, zkv), ones_kv], axis=1)]
            rows = []
            for g in range(SWA_GROUP):
                t0 = (pair * SWA_GROUP + g) * 2 * HEAD_DIM
                qt = q_ref[r0:r0 + WINDOW, t0:t0 + 2 * HEAD_DIM]
                zq = jnp.zeros_like(qt)
                rows.append(jnp.where(lane_q < HEAD_DIM, qt, zq))
                rows.append(jnp.where(lane_q >= HEAD_DIM, qt, zq))
            q8 = jnp.concatenate(rows, axis=0)
            s = lax.dot_general(q8, k2, (((1,), (1,)), ((), ())), preferred_element_type=F32)
            for g in range(SWA_GROUP):
                o = None
                for e in range(2):
                    rb = (g * 2 + e) * WINDOW
                    sb = s[rb:rb + WINDOW] + tbl_ref[pair, rb:rb + WINDOW, :]
                    if sub == 0:
                        sb = sb + jnp.where((kcol < WINDOW) & (blk == 0), NEG_INF, 0.0)
                    sink = jnp.full((WINDOW, lanes), sink_ref[(2 * pair + e) * SWA_GROUP + g] * LOG2E, F32)
                    m = jnp.maximum(jnp.max(sb, axis=1, keepdims=True), sink)
                    p = jnp.exp2(sb - jnp.concatenate([m, m], axis=1))
                    pv = jnp.dot(p.astype(BF16), v_aug[e], preferred_element_type=F32)
                    part = pv[:, :lanes] / (pv[:, lanes:] + jnp.exp2(sink - m))
                    o = part if o is None else o + part
                t0 = (pair * SWA_GROUP + g) * 2 * HEAD_DIM
                o_ref[r0:r0 + WINDOW, t0:t0 + 2 * HEAD_DIM] = o.astype(o_ref.dtype)


def _swa_attention(proj, sinks, tbl, *, batch, seq, tq):
    nq = seq // tq
    q_blk = (3 * DIFF_WIDTH) // SWA_WIDTH
    kv_w = SWA_KV_HEADS * HEAD_DIM
    k_blk = (3 * DIFF_WIDTH + SWA_WIDTH) // kv_w
    return pl.pallas_call(
        functools.partial(_swa_kernel, tq=tq),
        out_shape=jax.ShapeDtypeStruct((batch * seq, SWA_WIDTH), BF16),
        grid=(batch, nq),
        in_specs=[
            pl.BlockSpec(memory_space=pltpu.SMEM),
            pl.BlockSpec(tbl.shape, lambda b, i: (0, 0, 0)),
            pl.BlockSpec((tq, SWA_WIDTH), lambda b, i: (b * nq + i, q_blk)),
            pl.BlockSpec((seq, kv_w), lambda b, i: (b, k_blk)),
            pl.BlockSpec((seq, kv_w), lambda b, i: (b, k_blk + 1)),
        ],
        out_specs=pl.BlockSpec((tq, SWA_WIDTH), lambda b, i: (b * nq + i, 0)),
        compiler_params=pltpu.CompilerParams(
            dimension_semantics=("parallel", "arbitrary"),
            vmem_limit_bytes=VMEM_LIMIT_BYTES,
        ),
        name="swa_attention",
    )(sinks, tbl, proj, proj, proj)


def _outproj_kernel(x_ref, oa_ref, ob_ref, wa_ref, wb_ref, nw_ref, x1_ref, h_ref):
    y = jnp.dot(oa_ref[...], wa_ref[...], preferred_element_type=F32)
    y = y + jnp.dot(ob_ref[...], wb_ref[...], preferred_element_type=F32)
    x1 = x_ref[...] + y
    x1_ref[...] = x1
    h_ref[...] = (_rms_scale(x1) * nw_ref[...]).astype(h_ref.dtype)


def _outproj(x2, oa, ob, wa, wb, norm_w, *, tm):
    tokens, d = x2.shape
    return pl.pallas_call(
        _outproj_kernel,
        out_shape=(jax.ShapeDtypeStruct((tokens, d), F32), jax.ShapeDtypeStruct((tokens, d), BF16)),
        grid=(tokens // tm,),
        in_specs=[
            pl.BlockSpec((tm, d), lambda i: (i, 0)),
            pl.BlockSpec((tm, DIFF_WIDTH), lambda i: (i, 0)),
            pl.BlockSpec((tm, SWA_WIDTH), lambda i: (i, 0)),
            pl.BlockSpec((DIFF_WIDTH, d), lambda i: (0, 0)),
            pl.BlockSpec((SWA_WIDTH, d), lambda i: (0, 0)),
            pl.BlockSpec((1, d), lambda i: (0, 0)),
        ],
        out_specs=(pl.BlockSpec((tm, d), lambda i: (i, 0)), pl.BlockSpec((tm, d), lambda i: (i, 0))),
        compiler_params=pltpu.CompilerParams(
            dimension_semantics=("parallel",),
            vmem_limit_bytes=VMEM_LIMIT_BYTES,
        ),
        name="outproj_residual_norm",
    )(x2, oa, ob, wa, wb, norm_w)


def _ffn_kernel(h_ref, x1_ref, wg_ref, wu_ref, wd_ref, nw_ref, o_ref, *, rc):
    f = pl.program_id(1)

    @pl.when(f == 0)
    def _():
        o_ref[...] = x1_ref[...]

    def gated(c):
        h = h_ref[c * rc:(c + 1) * rc, :]
        g = jnp.dot(h, wg_ref[...], preferred_element_type=F32)
        u = jnp.dot(h, wu_ref[...], preferred_element_type=F32)
        return (g * (1.0 / (1.0 + jnp.exp(-g))) * u).astype(BF16)

    n_chunk = h_ref.shape[0] // rc
    a_next = gated(0)
    for c in range(n_chunk):
        a = a_next
        if c + 1 < n_chunk:
            a_next = gated(c + 1)
        o_ref[c * rc:(c + 1) * rc, :] += jnp.dot(a, wd_ref[...], preferred_element_type=F32)

    @pl.when(f == pl.num_programs(1) - 1)
    def _():
        o_ref[...] = _rms_scale(o_ref[...]) * nw_ref[...]


def _ffn(h2, x1, wg, wu, wd, norm_w, *, tm, tf, rc):
    tokens, d = x1.shape
    dff = wg.shape[1]
    return pl.pallas_call(
        functools.partial(_ffn_kernel, rc=rc),
        out_shape=jax.ShapeDtypeStruct((tokens, d), F32),
        grid=(tokens // tm, dff // tf),
        in_specs=[
            pl.BlockSpec((tm, d), lambda i, f: (i, 0)),
            pl.BlockSpec((tm, d), lambda i, f: (i, 0)),
            pl.BlockSpec((d, tf), lambda i, f: (0, f)),
            pl.BlockSpec((d, tf), lambda i, f: (0, f)),
            pl.BlockSpec((tf, d), lambda i, f: (f, 0)),
            pl.BlockSpec((1, d), lambda i, f: (0, 0)),
        ],
        out_specs=pl.BlockSpec((tm, d), lambda i, f: (i, 0)),
        compiler_params=pltpu.CompilerParams(
            dimension_semantics=("parallel", "arbitrary"),
            vmem_limit_bytes=VMEM_LIMIT_BYTES,
        ),
        name="swiglu_ffn_final_norm",
    )(h2, x1, wg, wu, wd, norm_w)


def kernel(x, attn_norm_w, w_in, lambda_q1, lambda_k1, lambda_q2, lambda_k2, subln_w, sinks, w_out,
           ffn_norm_w, w_gate, w_up, w_down, final_norm_w):
    batch, seq, d = x.shape
    depth = w_in.shape[0]
    assert (d, w_in.shape[2], w_gate.shape[2]) == (D_MODEL, IN_COLS, D_FF)
    tokens = batch * seq
    x2 = x.reshape(tokens, d)

    q_cols = np.ones((1, IN_COLS), np.float32)
    q_cols[:, :DIFF_WIDTH] = Q_SCALE
    q_cols[:, 3 * DIFF_WIDTH:3 * DIFF_WIDTH + SWA_WIDTH] = Q_SCALE
    col_scale = jnp.asarray(q_cols)
    diff_slopes2 = jnp.asarray((_alibi_slopes(DIFF_HEADS) * LOG2E).astype(np.float32))
    swa_tbl = jnp.asarray(_swa_tables())
    n_pair = SWA_KV_HEADS // 2

    assert depth == 1
    l = 0
    lambda_init = 0.8 - 0.6 * math.exp(-0.3 * l)
    wq = w_in[l][:, 3 * DIFF_WIDTH:3 * DIFF_WIDTH + SWA_WIDTH]
    wq = wq.reshape(d, n_pair, 2, SWA_GROUP, HEAD_DIM).transpose(0, 1, 3, 2, 4).reshape(d, SWA_WIDTH)
    w_in_b = jnp.concatenate(
        [w_in[l][:, :3 * DIFF_WIDTH], wq, w_in[l][:, 3 * DIFF_WIDTH + SWA_WIDTH:]], axis=1).astype(BF16)
    wo_a = w_out[l][:DIFF_WIDTH].astype(BF16)
    wo_b = w_out[l][DIFF_WIDTH:].reshape(n_pair, 2, SWA_GROUP, HEAD_DIM, d).transpose(0, 2, 1, 3, 4)
    wo_b = wo_b.reshape(SWA_WIDTH, d).astype(BF16)

    proj = _norm_inproj(x2, attn_norm_w[l].reshape(1, d), col_scale, w_in_b, tm=512, tn=512)
    oa = _diff_attention(
        proj, diff_slopes2,
        lambda_q1[l].reshape(1, HEAD_DIM), lambda_k1[l].reshape(1, HEAD_DIM),
        lambda_q2[l].reshape(1, HEAD_DIM), lambda_k2[l].reshape(1, HEAD_DIM),
        subln_w[l].reshape(1, 2 * HEAD_DIM),
        batch=batch, seq=seq, hp=1, tk=512, rc=256, lambda_init=lambda_init)
    ob = _swa_attention(proj, sinks[l], swa_tbl, batch=batch, seq=seq, tq=512)
    x1, h2 = _outproj(x2, oa, ob, wo_a, wo_b, ffn_norm_w[l].reshape(1, d), tm=512)
    out = _ffn(h2, x1, w_gate[l].astype(BF16), w_up[l].astype(BF16), w_down[l].astype(BF16),
               final_norm_w.reshape(1, d), tm=512, tf=512, rc=256)
    return out.reshape(batch, seq, d)
```

```python
import functools
import math

import jax
import jax.numpy as jnp
import numpy as np
from jax import lax
from jax.experimental import pallas as pl
from jax.experimental.pallas import tpu as pltpu

D_MODEL = 2048
HEAD_DIM = 64
DIFF_HEADS = 8
DIFF_WIDTH = 1024
SWA_Q_HEADS = 16
SWA_KV_HEADS = 4
SWA_GROUP = 4
SWA_WIDTH = 1024
WINDOW = 128
D_FF = 5632
IN_COLS = 4608
RMS_EPS = 1e-5
LOG2E = math.log2(math.e)
Q_SCALE = HEAD_DIM ** -0.5 * LOG2E
NEG_INF = float("-inf")

V7X_VMEM_BYTES = 64 * 1024 * 1024
VMEM_LIMIT_BYTES = 56 * 1024 * 1024

F32 = jnp.float32
BF16 = jnp.bfloat16


def _alibi_slopes(n_heads):
    return np.array([2.0 ** (-8.0 * (h + 1) / n_heads) for h in range(n_heads)], dtype=np.float64)


def _rms_scale(x):
    return x * lax.rsqrt(jnp.mean(x * x, axis=-1, keepdims=True) + RMS_EPS)


def _norm_inproj_kernel(x_ref, nw_ref, cs_ref, wa_ref, wq_ref, wkv_ref, wg_ref, wu_ref, wd_ref, wo_ref,
                        o_ref, wg_o, wu_o, wd_o, wo_o, h_ref, *, tn):
    h_ref[...] = (_rms_scale(x_ref[...]) * nw_ref[...]).astype(BF16)
    col = 0
    for w_ref in (wa_ref, wq_ref, wkv_ref):
        for j in range(w_ref.shape[1] // tn):
            cols = slice(col, col + tn)
            acc = jnp.dot(h_ref[...], w_ref[:, j * tn:(j + 1) * tn], preferred_element_type=F32)
            o_ref[:, cols] = (acc * cs_ref[:, cols]).astype(o_ref.dtype)
            col += tn
    wg_o[...] = wg_ref[...].astype(BF16)
    wu_o[...] = wu_ref[...].astype(BF16)
    wd_o[...] = wd_ref[...].astype(BF16)
    wo_o[...] = wo_ref[...].astype(BF16)


def _swa_out_row_block(i):
    n_diff = DIFF_WIDTH // HEAD_DIM
    t = i - n_diff
    per_pair = 2 * SWA_GROUP
    src = n_diff + (t // per_pair) * per_pair + (t % 2) * SWA_GROUP + (t % per_pair) // 2
    return jnp.where(i < n_diff, i, src)


def _norm_inproj(x2, norm_w, col_scale, w_in_b, wq_b, w_gate, w_up, w_down, w_out, *, tm, tn):
    tokens, d = x2.shape
    n = w_in_b.shape[1]
    dff = w_gate.shape[1]
    steps = tokens // tm
    assert w_out.shape[0] == steps * HEAD_DIM and d % steps == 0 and dff % steps == 0
    n_a = 3 * DIFF_WIDTH
    resident = dict(pipeline_mode=pl.Buffered(1))
    rows = lambda r, c: pl.BlockSpec((r, c), lambda i: (i, 0))
    return pl.pallas_call(
        functools.partial(_norm_inproj_kernel, tn=tn),
        out_shape=(
            jax.ShapeDtypeStruct((tokens, n), BF16),
            jax.ShapeDtypeStruct(w_gate.shape, BF16),
            jax.ShapeDtypeStruct(w_up.shape, BF16),
            jax.ShapeDtypeStruct(w_down.shape, BF16),
            jax.ShapeDtypeStruct(w_out.shape, BF16),
        ),
        grid=(steps,),
        in_specs=[
            pl.BlockSpec((tm, d), lambda i: (i, 0)),
            pl.BlockSpec((1, d), lambda i: (0, 0), **resident),
            pl.BlockSpec((1, n), lambda i: (0, 0), **resident),
            pl.BlockSpec((d, n_a), lambda i: (0, 0), **resident),
            pl.BlockSpec((d, SWA_WIDTH), lambda i: (0, 0), **resident),
            pl.BlockSpec((d, n - n_a - SWA_WIDTH), lambda i: (0, (n_a + SWA_WIDTH) // (n - n_a - SWA_WIDTH)),
                         **resident),
            rows(d // steps, dff),
            rows(d // steps, dff),
            rows(dff // steps, d),
            pl.BlockSpec((HEAD_DIM, d), lambda i: (_swa_out_row_block(i), 0)),
        ],
        out_specs=(
            pl.BlockSpec((tm, n), lambda i: (i, 0)),
            rows(d // steps, dff),
            rows(d // steps, dff),
            rows(dff // steps, d),
            rows(HEAD_DIM, d),
        ),
        scratch_shapes=[pltpu.VMEM((tm, d), BF16)],
        compiler_params=pltpu.CompilerParams(
            dimension_semantics=("parallel",),
            vmem_limit_bytes=VMEM_LIMIT_BYTES,
        ),
        name="norm_inproj",
    )(x2, norm_w, col_scale, w_in_b, wq_b, w_in_b, w_gate, w_up, w_down, w_out)


def _diff_attn_kernel(slope_ref, lq1_ref, lk1_ref, lq2_ref, lk2_ref, q_ref, k_ref, v_ref, sw_ref,
                      o_ref, m_ref, acc_ref, *, hp, tk, rc, lambda_init):
    seq = q_ref.shape[0]
    lanes = 2 * HEAD_DIM
    n_blk = seq // tk
    nt = (((1,), (1,)), ((), ()))
    slope2 = [slope_ref[pl.program_id(1) * hp + hh] for hh in range(hp)]

    first_map = lax.broadcasted_iota(jnp.int32, (rc, lanes), 1) < HEAD_DIM
    kcol = lax.broadcasted_iota(jnp.int32, (1, tk), 1).astype(F32)
    ones = jnp.ones((tk, lanes), BF16)
    tri = lax.broadcasted_iota(jnp.int32, (rc, rc), 1) <= lax.broadcasted_iota(jnp.int32, (rc, rc), 0)
    lam = (jnp.exp(jnp.sum(lq1_ref[...] * lk1_ref[...], axis=-1, keepdims=True))
           - jnp.exp(jnp.sum(lq2_ref[...] * lk2_ref[...], axis=-1, keepdims=True))
           + lambda_init)

    steps = [(hh, i, j, sub, mp) for hh in range(hp) for j in range(n_blk) for i in range(j, n_blk)
             for sub in range(tk // rc) for mp in range(2)]

    def n_cols(i, j, sub):
        return sub * rc + rc if i == j else tk

    def logits(hh, i, j, sub, mp):
        k0, r0, ncol = j * tk, i * tk + sub * rc, n_cols(i, j, sub)
        hl = slice(hh * lanes, (hh + 1) * lanes)
        qc = q_ref[r0:r0 + rc, hl]
        qz = jnp.where(first_map if mp == 0 else ~first_map, qc, jnp.zeros_like(qc))
        bias = slope2[hh] * (kcol[:, :ncol] + float(k0 - i * tk))
        s = lax.dot_general(qz, k_ref[k0:k0 + ncol, hl], nt, preferred_element_type=F32) + bias
        if i == j:
            s_diag = jnp.where(tri, s[:, ncol - rc:], NEG_INF)
            s = jnp.concatenate([s[:, :ncol - rc], s_diag], axis=1) if ncol > rc else s_diag
        return s

    def update(hh, i, j, sub, mp, s, o1):
        k0, r0, ncol = j * tk, i * tk + sub * rc, n_cols(i, j, sub)
        hl = slice(hh * lanes, (hh + 1) * lanes)
        vaug = jnp.concatenate([v_ref[k0:k0 + ncol, hl], ones[:ncol]], axis=1)
        row0 = (2 * hh + mp) * seq + r0
        rows = slice(row0, row0 + rc)
        m_cur = jnp.max(s, axis=1, keepdims=True)
        if j == 0:
            m_new = jnp.broadcast_to(m_cur, (rc, lanes))
        else:
            m_prev = m_ref[rows, :]
            m_new = jnp.maximum(m_prev, m_cur)
        p = jnp.exp2(s - jnp.concatenate([m_new] * (ncol // lanes), axis=1))
        pv = jnp.dot(p.astype(BF16), vaug, preferred_element_type=F32)
        if j > 0:
            alpha = jnp.exp2(m_prev - m_new)
            pv = jnp.concatenate([alpha, alpha], axis=1) * acc_ref[rows, :] + pv
        if i > j:
            m_ref[rows, :] = m_new
            acc_ref[rows, :] = pv
            return None
        if mp == 0:
            return pv[:, :lanes] / pv[:, lanes:]
        o = o1 - lam * (pv[:, :lanes] / pv[:, lanes:])
        o = _rms_scale(o) * sw_ref[...] * (1.0 - lambda_init)
        o_ref[r0:r0 + rc, hl] = o.astype(o_ref.dtype)
        return None

    ahead = 2
    pending = [logits(*st) for st in steps[:ahead]]
    o1 = None
    for n, step in enumerate(steps):
        s_cur = pending.pop(0)
        if n + ahead < len(steps):
            pending.append(logits(*steps[n + ahead]))
        o1 = update(*step, s_cur, o1)


def _diff_attention(proj, slopes2, lq1, lk1, lq2, lk2, subln_w, *, batch, seq, hp, tk, rc, lambda_init):
    lane_blk = hp * 2 * HEAD_DIM
    k_blk0 = DIFF_WIDTH // lane_blk
    v_blk0 = 2 * DIFF_WIDTH // lane_blk
    vec = lambda: pl.BlockSpec((1, HEAD_DIM), lambda b, h: (0, 0))
    return pl.pallas_call(
        functools.partial(_diff_attn_kernel, hp=hp, tk=tk, rc=rc, lambda_init=lambda_init),
        out_shape=jax.ShapeDtypeStruct((batch * seq, DIFF_WIDTH), BF16),
        grid=(batch, DIFF_HEADS // hp),
        in_specs=[
            pl.BlockSpec(memory_space=pltpu.SMEM),
            vec(), vec(), vec(), vec(),
            pl.BlockSpec((seq, lane_blk), lambda b, h: (b, h)),
            pl.BlockSpec((seq, lane_blk), lambda b, h: (b, k_blk0 + h)),
            pl.BlockSpec((seq, lane_blk), lambda b, h: (b, v_blk0 + h)),
            pl.BlockSpec((1, 2 * HEAD_DIM), lambda b, h: (0, 0)),
        ],
        out_specs=pl.BlockSpec((seq, lane_blk), lambda b, h: (b, h)),
        scratch_shapes=[
            pltpu.VMEM((hp * 2 * seq, 2 * HEAD_DIM), F32),
            pltpu.VMEM((hp * 2 * seq, 4 * HEAD_DIM), F32),
        ],
        compiler_params=pltpu.CompilerParams(
            dimension_semantics=("parallel", "parallel"),
            vmem_limit_bytes=VMEM_LIMIT_BYTES,
        ),
        name="diff_attention",
    )(slopes2, lq1, lk1, lq2, lk2, proj, proj, proj, subln_w)


def _swa_tables():
    slopes = _alibi_slopes(SWA_Q_HEADS)
    i = np.arange(WINDOW)[:, None]
    j = np.arange(2 * WINDOW)[None, :]
    dist = WINDOW + i - j
    valid = (dist >= 0) & (dist < WINDOW)
    tbl = np.empty((SWA_KV_HEADS // 2, SWA_GROUP * 2 * WINDOW, 2 * WINDOW), np.float32)
    for pair in range(SWA_KV_HEADS // 2):
        for g in range(SWA_GROUP):
            for e in range(2):
                head = (2 * pair + e) * SWA_GROUP + g
                r0 = (g * 2 + e) * WINDOW
                tbl[pair, r0:r0 + WINDOW] = np.where(valid, -slopes[head] * LOG2E * dist, -np.inf)
    return tbl


def _swa_kernel(sink_ref, tbl_ref, q_ref, k_ref, v_ref, o_ref, *, tq):
    qi = pl.program_id(1)
    n_sub = tq // WINDOW
    lanes = 2 * HEAD_DIM
    lane_q = lax.broadcasted_iota(jnp.int32, (WINDOW, lanes), 1)
    lane_kv = lax.broadcasted_iota(jnp.int32, (2 * WINDOW, lanes), 1)
    ones_kv = jnp.ones((2 * WINDOW, lanes), BF16)
    kcol = lax.broadcasted_iota(jnp.int32, (1, 2 * WINDOW), 1)
    for sub in range(n_sub):
        r0 = sub * WINDOW
        blk = qi * n_sub + sub
        cur0 = pl.multiple_of(blk * WINDOW, WINDOW)
        prev0 = pl.multiple_of(jnp.maximum(blk - 1, 0) * WINDOW, WINDOW)
        for pair in range(SWA_KV_HEADS // 2):
            c0 = pair * 2 * HEAD_DIM
            k2 = jnp.concatenate([k_ref[pl.ds(prev0, WINDOW), c0:c0 + 2 * HEAD_DIM],
                                  k_ref[pl.ds(cur0, WINDOW), c0:c0 + 2 * HEAD_DIM]], axis=0)
            v2 = jnp.concatenate([v_ref[pl.ds(prev0, WINDOW), c0:c0 + 2 * HEAD_DIM],
                                  v_ref[pl.ds(cur0, WINDOW), c0:c0 + 2 * HEAD_DIM]], axis=0)
            zkv = jnp.zeros_like(v2)
            v_aug = [jnp.concatenate([jnp.where(lane_kv < HEAD_DIM, v2, zkv), ones_kv], axis=1),
                     jnp.concatenate([jnp.where(lane_kv >= HEAD_DIM, v2, zkv), ones_kv], axis=1)]
            rows = []
            for g in range(SWA_GROUP):
                t0 = (pair * SWA_GROUP + g) * 2 * HEAD_DIM
                qt = q_ref[r0:r0 + WINDOW, t0:t0 + 2 * HEAD_DIM]
                zq = jnp.zeros_like(qt)
                rows.append(jnp.where(lane_q < HEAD_DIM, qt, zq))
                rows.append(jnp.where(lane_q >= HEAD_DIM, qt, zq))
            q8 = jnp.concatenate(rows, axis=0)
            s = lax.dot_general(q8, k2, (((1,), (1,)), ((), ())), preferred_element_type=F32)
            for g in range(SWA_GROUP):
                o = None
                for e in range(2):
                    rb = (g * 2 + e) * WINDOW
                    sb = s[rb:rb + WINDOW] + tbl_ref[pair, rb:rb + WINDOW, :]
                    if sub == 0:
                        sb = sb + jnp.where((kcol < WINDOW) & (blk == 0), NEG_INF, 0.0)
                    sink = jnp.full((WINDOW, lanes), sink_ref[(2 * pair + e) * SWA_GROUP + g] * LOG2E, F32)
                    m = jnp.maximum(jnp.max(sb, axis=1, keepdims=True), sink)
                    p = jnp.exp2(sb - jnp.concatenate([m, m], axis=1))
                    pv = jnp.dot(p.astype(BF16), v_aug[e], preferred_element_type=F32)
                    part = pv[:, :lanes] / (pv[:, lanes:] + jnp.exp2(sink - m))
                    o = part if o is None else o + part
                t0 = (pair * SWA_GROUP + g) * 2 * HEAD_DIM
                o_ref[r0:r0 + WINDOW, t0:t0 + 2 * HEAD_DIM] = o.astype(o_ref.dtype)


def _swa_attention(proj, sinks, tbl, *, batch, seq, tq):
    nq = seq // tq
    q_blk = (3 * DIFF_WIDTH) // SWA_WIDTH
    kv_w = SWA_KV_HEADS * HEAD_DIM
    k_blk = (3 * DIFF_WIDTH + SWA_WIDTH) // kv_w
    return pl.pallas_call(
        functools.partial(_swa_kernel, tq=tq),
        out_shape=jax.ShapeDtypeStruct((batch * seq, SWA_WIDTH), BF16),
        grid=(batch, nq),
        in_specs=[
            pl.BlockSpec(memory_space=pltpu.SMEM),
            pl.BlockSpec(tbl.shape, lambda b, i: (0, 0, 0)),
            pl.BlockSpec((tq, SWA_WIDTH), lambda b, i: (b * nq + i, q_blk)),
            pl.BlockSpec((seq, kv_w), lambda b, i: (b, k_blk)),
            pl.BlockSpec((seq, kv_w), lambda b, i: (b, k_blk + 1)),
        ],
        out_specs=pl.BlockSpec((tq, SWA_WIDTH), lambda b, i: (b * nq + i, 0)),
        compiler_params=pltpu.CompilerParams(
            dimension_semantics=("parallel", "arbitrary"),
            vmem_limit_bytes=VMEM_LIMIT_BYTES,
        ),
        name="swa_attention",
    )(sinks, tbl, proj, proj, proj)


def _outproj_kernel(x_ref, oa_ref, ob_ref, wa_ref, wb_ref, nw_ref, x1_ref, h_ref, *, rc):
    for c in range(x_ref.shape[0] // rc):
        rows = slice(c * rc, (c + 1) * rc)
        y = jnp.dot(oa_ref[rows, :], wa_ref[...], preferred_element_type=F32)
        y = y + jnp.dot(ob_ref[rows, :], wb_ref[...], preferred_element_type=F32)
        x1 = x_ref[rows, :] + y
        x1_ref[rows, :] = x1
        h_ref[rows, :] = (_rms_scale(x1) * nw_ref[...]).astype(h_ref.dtype)


def _outproj(x2, oa, ob, w, norm_w, *, tm, rc):
    tokens, d = x2.shape
    resident = dict(pipeline_mode=pl.Buffered(1))
    return pl.pallas_call(
        functools.partial(_outproj_kernel, rc=rc),
        out_shape=(jax.ShapeDtypeStruct((tokens, d), F32), jax.ShapeDtypeStruct((tokens, d), BF16)),
        grid=(tokens // tm,),
        in_specs=[
            pl.BlockSpec((tm, d), lambda i: (i, 0)),
            pl.BlockSpec((tm, DIFF_WIDTH), lambda i: (i, 0)),
            pl.BlockSpec((tm, SWA_WIDTH), lambda i: (i, 0)),
            pl.BlockSpec((DIFF_WIDTH, d), lambda i: (0, 0), **resident),
            pl.BlockSpec((SWA_WIDTH, d), lambda i: (DIFF_WIDTH // SWA_WIDTH, 0), **resident),
            pl.BlockSpec((1, d), lambda i: (0, 0), **resident),
        ],
        out_specs=(pl.BlockSpec((tm, d), lambda i: (i, 0)), pl.BlockSpec((tm, d), lambda i: (i, 0))),
        compiler_params=pltpu.CompilerParams(
            dimension_semantics=("parallel",),
            vmem_limit_bytes=VMEM_LIMIT_BYTES,
        ),
        name="outproj_residual_norm",
    )(x2, oa, ob, w, w, norm_w)


def _ffn_kernel(h_ref, x1_ref, wg_ref, wu_ref, wd_ref, nw_ref, o_ref, *, rc):
    f = pl.program_id(1)

    @pl.when(f == 0)
    def _():
        o_ref[...] = x1_ref[...]

    def gated(c):
        h = h_ref[c * rc:(c + 1) * rc, :]
        g = jnp.dot(h, wg_ref[...], preferred_element_type=F32)
        u = jnp.dot(h, wu_ref[...], preferred_element_type=F32)
        return (g * (1.0 / (1.0 + jnp.exp(-g))) * u).astype(BF16)

    n_chunk = h_ref.shape[0] // rc
    a_next = gated(0)
    for c in range(n_chunk):
        a = a_next
        if c + 1 < n_chunk:
            a_next = gated(c + 1)
        o_ref[c * rc:(c + 1) * rc, :] += jnp.dot(a, wd_ref[...], preferred_element_type=F32)

    @pl.when(f == pl.num_programs(1) - 1)
    def _():
        o_ref[...] = _rms_scale(o_ref[...]) * nw_ref[...]


def _ffn(h2, x1, wg, wu, wd, norm_w, *, tm, tf, rc):
    tokens, d = x1.shape
    dff = wg.shape[1]
    return pl.pallas_call(
        functools.partial(_ffn_kernel, rc=rc),
        out_shape=jax.ShapeDtypeStruct((tokens, d), F32),
        grid=(tokens // tm, dff // tf),
        in_specs=[
            pl.BlockSpec((tm, d), lambda i, f: (i, 0)),
            pl.BlockSpec((tm, d), lambda i, f: (i, 0)),
            pl.BlockSpec((d, tf), lambda i, f: (0, f)),
            pl.BlockSpec((d, tf), lambda i, f: (0, f)),
            pl.BlockSpec((tf, d), lambda i, f: (f, 0)),
            pl.BlockSpec((1, d), lambda i, f: (0, 0)),
        ],
        out_specs=pl.BlockSpec((tm, d), lambda i, f: (i, 0)),
        compiler_params=pltpu.CompilerParams(
            dimension_semantics=("parallel", "arbitrary"),
            vmem_limit_bytes=VMEM_LIMIT_BYTES,
        ),
        name="swiglu_ffn_final_norm",
    )(h2, x1, wg, wu, wd, norm_w)


def kernel(x, attn_norm_w, w_in, lambda_q1, lambda_k1, lambda_q2, lambda_k2, subln_w, sinks, w_out,
           ffn_norm_w, w_gate, w_up, w_down, final_norm_w):
    batch, seq, d = x.shape
    depth = w_in.shape[0]
    assert (d, w_in.shape[2], w_gate.shape[2]) == (D_MODEL, IN_COLS, D_FF)
    tokens = batch * seq
    x2 = x.reshape(tokens, d)

    q_cols = np.ones((1, IN_COLS), np.float32)
    q_cols[:, :DIFF_WIDTH] = Q_SCALE
    q_cols[:, 3 * DIFF_WIDTH:3 * DIFF_WIDTH + SWA_WIDTH] = Q_SCALE
    col_scale = jnp.asarray(q_cols)
    diff_slopes2 = jnp.asarray((_alibi_slopes(DIFF_HEADS) * LOG2E).astype(np.float32))
    swa_tbl = jnp.asarray(_swa_tables())
    n_pair = SWA_KV_HEADS // 2

    assert depth == 1
    l = 0
    lambda_init = 0.8 - 0.6 * math.exp(-0.3 * l)
    wq = w_in[l][:, 3 * DIFF_WIDTH:3 * DIFF_WIDTH + SWA_WIDTH].astype(BF16)
    wq = wq.reshape(d, n_pair, 2, SWA_GROUP, HEAD_DIM).transpose(0, 1, 3, 2, 4).reshape(d, SWA_WIDTH)
    w_in_b = w_in[l].astype(BF16)

    proj, wg_b, wu_b, wd_b, wo_b = _norm_inproj(
        x2, attn_norm_w[l].reshape(1, d), col_scale, w_in_b, wq, w_gate[l], w_up[l], w_down[l], w_out[l],
        tm=512, tn=512)
    oa = _diff_attention(
        proj, diff_slopes2,
        lambda_q1[l].reshape(1, HEAD_DIM), lambda_k1[l].reshape(1, HEAD_DIM),
        lambda_q2[l].reshape(1, HEAD_DIM), lambda_k2[l].reshape(1, HEAD_DIM),
        subln_w[l].reshape(1, 2 * HEAD_DIM),
        batch=batch, seq=seq, hp=1, tk=512, rc=256, lambda_init=lambda_init)
    ob = _swa_attention(proj, sinks[l], swa_tbl, batch=batch, seq=seq, tq=512)
    x1, h2 = _outproj(x2, oa, ob, wo_b, ffn_norm_w[l].reshape(1, d), tm=512, rc=256)
    out = _ffn(h2, x1, wg_b, wu_b, wd_b, final_norm_w.reshape(1, d), tm=512, tf=512, rc=256)
    return out.reshape(batch, seq, d)
```

```python
import functools
import math

import jax
import jax.numpy as jnp
import numpy as np
from jax import lax
from jax.experimental import pallas as pl
from jax.experimental.pallas import tpu as pltpu

D_MODEL = 2048
HEAD_DIM = 64
DIFF_HEADS = 8
DIFF_WIDTH = 1024
SWA_Q_HEADS = 16
SWA_KV_HEADS = 4
SWA_GROUP = 4
SWA_WIDTH = 1024
WINDOW = 128
D_FF = 5632
IN_COLS = 4608
RMS_EPS = 1e-5
LOG2E = math.log2(math.e)
Q_SCALE = HEAD_DIM ** -0.5 * LOG2E
NEG_INF = float("-inf")

V7X_VMEM_BYTES = 64 * 1024 * 1024
VMEM_LIMIT_BYTES = 56 * 1024 * 1024

F32 = jnp.float32
BF16 = jnp.bfloat16


def _alibi_slopes(n_heads):
    return np.array([2.0 ** (-8.0 * (h + 1) / n_heads) for h in range(n_heads)], dtype=np.float64)


def _rms_scale(x):
    return x * lax.rsqrt(jnp.mean(x * x, axis=-1, keepdims=True) + RMS_EPS)


def _norm_inproj_kernel(x_ref, nw_ref, cs_ref, wa_ref, wq_ref, wkv_ref, wg_ref, wu_ref, wd_ref, wo_ref,
                        o_ref, wg_o, wu_o, wd_o, wo_o, h_ref, *, tn):
    h_ref[...] = (_rms_scale(x_ref[...]) * nw_ref[...]).astype(BF16)
    col = 0
    for w_ref in (wa_ref, wq_ref, wkv_ref):
        for j in range(w_ref.shape[1] // tn):
            cols = slice(col, col + tn)
            acc = jnp.dot(h_ref[...], w_ref[:, j * tn:(j + 1) * tn], preferred_element_type=F32)
            o_ref[:, cols] = (acc * cs_ref[:, cols]).astype(o_ref.dtype)
            col += tn
    tf = wg_o.shape[2]
    for f in range(wg_o.shape[0]):
        wg_o[f] = wg_ref[:, f * tf:(f + 1) * tf].astype(BF16)
        wu_o[f] = wu_ref[:, f * tf:(f + 1) * tf].astype(BF16)
    wd_o[...] = wd_ref[...].astype(BF16)
    wo_o[...] = wo_ref[...].astype(BF16)


def _swa_out_row_block(i):
    n_diff = DIFF_WIDTH // HEAD_DIM
    t = i - n_diff
    per_pair = 2 * SWA_GROUP
    src = n_diff + (t // per_pair) * per_pair + (t % 2) * SWA_GROUP + (t % per_pair) // 2
    return jnp.where(i < n_diff, i, src)


def _norm_inproj(x2, norm_w, col_scale, w_in_b, wq_b, w_gate, w_up, w_down, w_out, *, tm, tn, tf):
    tokens, d = x2.shape
    n = w_in_b.shape[1]
    dff = w_gate.shape[1]
    steps = tokens // tm
    assert w_out.shape[0] == steps * HEAD_DIM and d % steps == 0 and dff % steps == 0
    n_a = 3 * DIFF_WIDTH
    resident = dict(pipeline_mode=pl.Buffered(1))
    rows = lambda r, c: pl.BlockSpec((r, c), lambda i: (i, 0))
    col_blocks = lambda: pl.BlockSpec((dff // tf, d // steps, tf), lambda i: (0, i, 0))
    return pl.pallas_call(
        functools.partial(_norm_inproj_kernel, tn=tn),
        out_shape=(
            jax.ShapeDtypeStruct((tokens, n), BF16),
            jax.ShapeDtypeStruct((dff // tf, d, tf), BF16),
            jax.ShapeDtypeStruct((dff // tf, d, tf), BF16),
            jax.ShapeDtypeStruct(w_down.shape, BF16),
            jax.ShapeDtypeStruct(w_out.shape, BF16),
        ),
        grid=(steps,),
        in_specs=[
            pl.BlockSpec((tm, d), lambda i: (i, 0)),
            pl.BlockSpec((1, d), lambda i: (0, 0), **resident),
            pl.BlockSpec((1, n), lambda i: (0, 0), **resident),
            pl.BlockSpec((d, n_a), lambda i: (0, 0), **resident),
            pl.BlockSpec((d, SWA_WIDTH), lambda i: (0, 0), **resident),
            pl.BlockSpec((d, n - n_a - SWA_WIDTH), lambda i: (0, (n_a + SWA_WIDTH) // (n - n_a - SWA_WIDTH)),
                         **resident),
            rows(d // steps, dff),
            rows(d // steps, dff),
            rows(dff // steps, d),
            pl.BlockSpec((HEAD_DIM, d), lambda i: (_swa_out_row_block(i), 0)),
        ],
        out_specs=(
            pl.BlockSpec((tm, n), lambda i: (i, 0)),
            col_blocks(),
            col_blocks(),
            rows(dff // steps, d),
            rows(HEAD_DIM, d),
        ),
        scratch_shapes=[pltpu.VMEM((tm, d), BF16)],
        compiler_params=pltpu.CompilerParams(
            dimension_semantics=("parallel",),
            vmem_limit_bytes=VMEM_LIMIT_BYTES,
        ),
        name="norm_inproj",
    )(x2, norm_w, col_scale, w_in_b, wq_b, w_in_b, w_gate, w_up, w_down, w_out)


def _diff_attn_kernel(slope_ref, lq1_ref, lk1_ref, lq2_ref, lk2_ref, q_ref, k_ref, v_ref, sw_ref,
                      o_ref, m_ref, acc_ref, *, hp, tk, rc, lambda_init):
    seq = q_ref.shape[0]
    lanes = 2 * HEAD_DIM
    n_blk = seq // tk
    nt = (((1,), (1,)), ((), ()))
    slope2 = [slope_ref[pl.program_id(1) * hp + hh] for hh in range(hp)]

    first_map = lax.broadcasted_iota(jnp.int32, (rc, lanes), 1) < HEAD_DIM
    kcol = lax.broadcasted_iota(jnp.int32, (1, tk), 1).astype(F32)
    ones = jnp.ones((tk, lanes), BF16)
    tri = lax.broadcasted_iota(jnp.int32, (rc, rc), 1) <= lax.broadcasted_iota(jnp.int32, (rc, rc), 0)
    lam = (jnp.exp(jnp.sum(lq1_ref[...] * lk1_ref[...], axis=-1, keepdims=True))
           - jnp.exp(jnp.sum(lq2_ref[...] * lk2_ref[...], axis=-1, keepdims=True))
           + lambda_init)

    steps = [(hh, i, j, sub, mp) for hh in range(hp) for j in range(n_blk) for i in range(j, n_blk)
             for sub in range(tk // rc) for mp in range(2)]

    def n_cols(i, j, sub):
        return sub * rc + rc if i == j else tk

    def logits(hh, i, j, sub, mp):
        k0, r0, ncol = j * tk, i * tk + sub * rc, n_cols(i, j, sub)
        hl = slice(hh * lanes, (hh + 1) * lanes)
        qc = q_ref[r0:r0 + rc, hl]
        qz = jnp.where(first_map if mp == 0 else ~first_map, qc, jnp.zeros_like(qc))
        bias = slope2[hh] * (kcol[:, :ncol] + float(k0 - i * tk))
        s = lax.dot_general(qz, k_ref[k0:k0 + ncol, hl], nt, preferred_element_type=F32) + bias
        if i == j:
            s_diag = jnp.where(tri, s[:, ncol - rc:], NEG_INF)
            s = jnp.concatenate([s[:, :ncol - rc], s_diag], axis=1) if ncol > rc else s_diag
        return s

    def update(hh, i, j, sub, mp, s, o1):
        k0, r0, ncol = j * tk, i * tk + sub * rc, n_cols(i, j, sub)
        hl = slice(hh * lanes, (hh + 1) * lanes)
        vaug = jnp.concatenate([v_ref[k0:k0 + ncol, hl], ones[:ncol]], axis=1)
        row0 = (2 * hh + mp) * seq + r0
        rows = slice(row0, row0 + rc)
        m_cur = jnp.max(s, axis=1, keepdims=True)
        if j == 0:
            m_new = jnp.broadcast_to(m_cur, (rc, lanes))
        else:
            m_prev = m_ref[rows, :]
            m_new = jnp.maximum(m_prev, m_cur)
        p = jnp.exp2(s - jnp.concatenate([m_new] * (ncol // lanes), axis=1))
        pv = jnp.dot(p.astype(BF16), vaug, preferred_element_type=F32)
        if j > 0:
            alpha = jnp.exp2(m_prev - m_new)
            pv = jnp.concatenate([alpha, alpha], axis=1) * acc_ref[rows, :] + pv
        if i > j:
            m_ref[rows, :] = m_new
            acc_ref[rows, :] = pv
            return None
        if mp == 0:
            return pv[:, :lanes] / pv[:, lanes:]
        o = o1 - lam * (pv[:, :lanes] / pv[:, lanes:])
        o = _rms_scale(o) * sw_ref[...] * (1.0 - lambda_init)
        o_ref[r0:r0 + rc, hl] = o.astype(o_ref.dtype)
        return None

    ahead = 2
    pending = [logits(*st) for st in steps[:ahead]]
    o1 = None
    for n, step in enumerate(steps):
        s_cur = pending.pop(0)
        if n + ahead < len(steps):
            pending.append(logits(*steps[n + ahead]))
        o1 = update(*step, s_cur, o1)


def _diff_attention(proj, slopes2, lq1, lk1, lq2, lk2, subln_w, *, batch, seq, hp, tk, rc, lambda_init):
    lane_blk = hp * 2 * HEAD_DIM
    k_blk0 = DIFF_WIDTH // lane_blk
    v_blk0 = 2 * DIFF_WIDTH // lane_blk
    vec = lambda: pl.BlockSpec((1, HEAD_DIM), lambda b, h: (0, 0))
    return pl.pallas_call(
        functools.partial(_diff_attn_kernel, hp=hp, tk=tk, rc=rc, lambda_init=lambda_init),
        out_shape=jax.ShapeDtypeStruct((batch * seq, DIFF_WIDTH), BF16),
        grid=(batch, DIFF_HEADS // hp),
        in_specs=[
            pl.BlockSpec(memory_space=pltpu.SMEM),
            vec(), vec(), vec(), vec(),
            pl.BlockSpec((seq, lane_blk), lambda b, h: (b, h)),
            pl.BlockSpec((seq, lane_blk), lambda b, h: (b, k_blk0 + h)),
            pl.BlockSpec((seq, lane_blk), lambda b, h: (b, v_blk0 + h)),
            pl.BlockSpec((1, 2 * HEAD_DIM), lambda b, h: (0, 0)),
        ],
        out_specs=pl.BlockSpec((seq, lane_blk), lambda b, h: (b, h)),
        scratch_shapes=[
            pltpu.VMEM((hp * 2 * seq, 2 * HEAD_DIM), F32),
            pltpu.VMEM((hp * 2 * seq, 4 * HEAD_DIM), F32),
        ],
        compiler_params=pltpu.CompilerParams(
            dimension_semantics=("parallel", "parallel"),
            vmem_limit_bytes=VMEM_LIMIT_BYTES,
        ),
        name="diff_attention",
    )(slopes2, lq1, lk1, lq2, lk2, proj, proj, proj, subln_w)


def _swa_tables():
    slopes = _alibi_slopes(SWA_Q_HEADS)
    i = np.arange(WINDOW)[:, None]
    j = np.arange(2 * WINDOW)[None, :]
    dist = WINDOW + i - j
    valid = (dist >= 0) & (dist < WINDOW)
    tbl = np.empty((SWA_KV_HEADS // 2, SWA_GROUP * 2 * WINDOW, 2 * WINDOW), np.float32)
    for pair in range(SWA_KV_HEADS // 2):
        for g in range(SWA_GROUP):
            for e in range(2):
                head = (2 * pair + e) * SWA_GROUP + g
                r0 = (g * 2 + e) * WINDOW
                tbl[pair, r0:r0 + WINDOW] = np.where(valid, -slopes[head] * LOG2E * dist, -np.inf)
    return tbl


def _swa_kernel(sink_ref, tbl_ref, q_ref, k_ref, v_ref, o_ref, *, tq):
    qi = pl.program_id(1)
    n_sub = tq // WINDOW
    lanes = 2 * HEAD_DIM
    lane_q = lax.broadcasted_iota(jnp.int32, (WINDOW, lanes), 1)
    lane_kv = lax.broadcasted_iota(jnp.int32, (2 * WINDOW, lanes), 1)
    ones_kv = jnp.ones((2 * WINDOW, lanes), BF16)
    kcol = lax.broadcasted_iota(jnp.int32, (1, 2 * WINDOW), 1)
    for sub in range(n_sub):
        r0 = sub * WINDOW
        blk = qi * n_sub + sub
        cur0 = pl.multiple_of(blk * WINDOW, WINDOW)
        prev0 = pl.multiple_of(jnp.maximum(blk - 1, 0) * WINDOW, WINDOW)
        for pair in range(SWA_KV_HEADS // 2):
            c0 = pair * 2 * HEAD_DIM
            k2 = jnp.concatenate([k_ref[pl.ds(prev0, WINDOW), c0:c0 + 2 * HEAD_DIM],
                                  k_ref[pl.ds(cur0, WINDOW), c0:c0 + 2 * HEAD_DIM]], axis=0)
            v2 = jnp.concatenate([v_ref[pl.ds(prev0, WINDOW), c0:c0 + 2 * HEAD_DIM],
                                  v_ref[pl.ds(cur0, WINDOW), c0:c0 + 2 * HEAD_DIM]], axis=0)
            zkv = jnp.zeros_like(v2)
            v_aug = [jnp.concatenate([jnp.where(lane_kv < HEAD_DIM, v2, zkv), ones_kv], axis=1),
                     jnp.concatenate([jnp.where(lane_kv >= HEAD_DIM, v2, zkv), ones_kv], axis=1)]
            rows = []
            for g in range(SWA_GROUP):
                t0 = (pair * SWA_GROUP + g) * 2 * HEAD_DIM
                qt = q_ref[r0:r0 + WINDOW, t0:t0 + 2 * HEAD_DIM]
                zq = jnp.zeros_like(qt)
                rows.append(jnp.where(lane_q < HEAD_DIM, qt, zq))
                rows.append(jnp.where(lane_q >= HEAD_DIM, qt, zq))
            q8 = jnp.concatenate(rows, axis=0)
            s = lax.dot_general(q8, k2, (((1,), (1,)), ((), ())), preferred_element_type=F32)
            for g in range(SWA_GROUP):
                o = None
                for e in range(2):
                    rb = (g * 2 + e) * WINDOW
                    sb = s[rb:rb + WINDOW] + tbl_ref[pair, rb:rb + WINDOW, :]
                    if sub == 0:
                        sb = sb + jnp.where((kcol < WINDOW) & (blk == 0), NEG_INF, 0.0)
                    sink = jnp.full((WINDOW, lanes), sink_ref[(2 * pair + e) * SWA_GROUP + g] * LOG2E, F32)
                    m = jnp.maximum(jnp.max(sb, axis=1, keepdims=True), sink)
                    p = jnp.exp2(sb - jnp.concatenate([m, m], axis=1))
                    pv = jnp.dot(p.astype(BF16), v_aug[e], preferred_element_type=F32)
                    part = pv[:, :lanes] / (pv[:, lanes:] + jnp.exp2(sink - m))
                    o = part if o is None else o + part
                t0 = (pair * SWA_GROUP + g) * 2 * HEAD_DIM
                o_ref[r0:r0 + WINDOW, t0:t0 + 2 * HEAD_DIM] = o.astype(o_ref.dtype)


def _swa_attention(proj, sinks, tbl, *, batch, seq, tq):
    nq = seq // tq
    q_blk = (3 * DIFF_WIDTH) // SWA_WIDTH
    kv_w = SWA_KV_HEADS * HEAD_DIM
    k_blk = (3 * DIFF_WIDTH + SWA_WIDTH) // kv_w
    return pl.pallas_call(
        functools.partial(_swa_kernel, tq=tq),
        out_shape=jax.ShapeDtypeStruct((batch * seq, SWA_WIDTH), BF16),
        grid=(batch, nq),
        in_specs=[
            pl.BlockSpec(memory_space=pltpu.SMEM),
            pl.BlockSpec(tbl.shape, lambda b, i: (0, 0, 0)),
            pl.BlockSpec((tq, SWA_WIDTH), lambda b, i: (b * nq + i, q_blk)),
            pl.BlockSpec((seq, kv_w), lambda b, i: (b, k_blk)),
            pl.BlockSpec((seq, kv_w), lambda b, i: (b, k_blk + 1)),
        ],
        out_specs=pl.BlockSpec((tq, SWA_WIDTH), lambda b, i: (b * nq + i, 0)),
        compiler_params=pltpu.CompilerParams(
            dimension_semantics=("parallel", "arbitrary"),
            vmem_limit_bytes=VMEM_LIMIT_BYTES,
        ),
        name="swa_attention",
    )(sinks, tbl, proj, proj, proj)


def _outproj_kernel(x_ref, oa_ref, ob_ref, wa_ref, wb_ref, nw_ref, x1_ref, h_ref, *, rc):
    for c in range(x_ref.shape[0] // rc):
        rows = slice(c * rc, (c + 1) * rc)
        y = jnp.dot(oa_ref[rows, :], wa_ref[...], preferred_element_type=F32)
        y = y + jnp.dot(ob_ref[rows, :], wb_ref[...], preferred_element_type=F32)
        x1 = x_ref[rows, :] + y
        x1_ref[rows, :] = x1
        h_ref[rows, :] = (_rms_scale(x1) * nw_ref[...]).astype(h_ref.dtype)


def _outproj(x2, oa, ob, w, norm_w, *, tm, rc):
    tokens, d = x2.shape
    resident = dict(pipeline_mode=pl.Buffered(1))
    return pl.pallas_call(
        functools.partial(_outproj_kernel, rc=rc),
        out_shape=(jax.ShapeDtypeStruct((tokens, d), F32), jax.ShapeDtypeStruct((tokens, d), BF16)),
        grid=(tokens // tm,),
        in_specs=[
            pl.BlockSpec((tm, d), lambda i: (i, 0)),
            pl.BlockSpec((tm, DIFF_WIDTH), lambda i: (i, 0)),
            pl.BlockSpec((tm, SWA_WIDTH), lambda i: (i, 0)),
            pl.BlockSpec((DIFF_WIDTH, d), lambda i: (0, 0), **resident),
            pl.BlockSpec((SWA_WIDTH, d), lambda i: (DIFF_WIDTH // SWA_WIDTH, 0), **resident),
            pl.BlockSpec((1, d), lambda i: (0, 0), **resident),
        ],
        out_specs=(pl.BlockSpec((tm, d), lambda i: (i, 0)), pl.BlockSpec((tm, d), lambda i: (i, 0))),
        compiler_params=pltpu.CompilerParams(
            dimension_semantics=("parallel",),
            vmem_limit_bytes=VMEM_LIMIT_BYTES,
        ),
        name="outproj_residual_norm",
    )(x2, oa, ob, w, w, norm_w)


def _ffn_kernel(h_ref, x1_ref, wg_ref, wu_ref, wd_ref, nw_ref, o_ref, *, rc):
    f = pl.program_id(1)

    @pl.when(f == 0)
    def _():
        o_ref[...] = x1_ref[...]

    def gated(c):
        h = h_ref[c * rc:(c + 1) * rc, :]
        g = jnp.dot(h, wg_ref[...], preferred_element_type=F32)
        u = jnp.dot(h, wu_ref[...], preferred_element_type=F32)
        return (g * (1.0 / (1.0 + jnp.exp(-g))) * u).astype(BF16)

    n_chunk = h_ref.shape[0] // rc
    a_next = gated(0)
    for c in range(n_chunk):
        a = a_next
        if c + 1 < n_chunk:
            a_next = gated(c + 1)
        o_ref[c * rc:(c + 1) * rc, :] += jnp.dot(a, wd_ref[...], preferred_element_type=F32)

    @pl.when(f == pl.num_programs(1) - 1)
    def _():
        o_ref[...] = _rms_scale(o_ref[...]) * nw_ref[...]


def _ffn(h2, x1, wg, wu, wd, norm_w, *, tm, tf, rc):
    tokens, d = x1.shape
    dff = wd.shape[0]
    return pl.pallas_call(
        functools.partial(_ffn_kernel, rc=rc),
        out_shape=jax.ShapeDtypeStruct((tokens, d), F32),
        grid=(tokens // tm, dff // tf),
        in_specs=[
            pl.BlockSpec((tm, d), lambda i, f: (i, 0)),
            pl.BlockSpec((tm, d), lambda i, f: (i, 0)),
            pl.BlockSpec((None, d, tf), lambda i, f: (f, 0, 0)),
            pl.BlockSpec((None, d, tf), lambda i, f: (f, 0, 0)),
            pl.BlockSpec((tf, d), lambda i, f: (f, 0)),
            pl.BlockSpec((1, d), lambda i, f: (0, 0)),
        ],
        out_specs=pl.BlockSpec((tm, d), lambda i, f: (i, 0)),
        compiler_params=pltpu.CompilerParams(
            dimension_semantics=("parallel", "arbitrary"),
            vmem_limit_bytes=VMEM_LIMIT_BYTES,
        ),
        name="swiglu_ffn_final_norm",
    )(h2, x1, wg, wu, wd, norm_w)


def kernel(x, attn_norm_w, w_in, lambda_q1, lambda_k1, lambda_q2, lambda_k2, subln_w, sinks, w_out,
           ffn_norm_w, w_gate, w_up, w_down, final_norm_w):
    batch, seq, d = x.shape
    depth = w_in.shape[0]
    assert (d, w_in.shape[2], w_gate.shape[2]) == (D_MODEL, IN_COLS, D_FF)
    tokens = batch * seq
    x2 = x.reshape(tokens, d)

    q_cols = np.ones((1, IN_COLS), np.float32)
    q_cols[:, :DIFF_WIDTH] = Q_SCALE
    q_cols[:, 3 * DIFF_WIDTH:3 * DIFF_WIDTH + SWA_WIDTH] = Q_SCALE
    col_scale = jnp.asarray(q_cols)
    diff_slopes2 = jnp.asarray((_alibi_slopes(DIFF_HEADS) * LOG2E).astype(np.float32))
    swa_tbl = jnp.asarray(_swa_tables())
    n_pair = SWA_KV_HEADS // 2

    assert depth == 1
    l = 0
    lambda_init = 0.8 - 0.6 * math.exp(-0.3 * l)
    wq = w_in[l][:, 3 * DIFF_WIDTH:3 * DIFF_WIDTH + SWA_WIDTH].astype(BF16)
    wq = wq.reshape(d, n_pair, 2, SWA_GROUP, HEAD_DIM).transpose(0, 1, 3, 2, 4).reshape(d, SWA_WIDTH)
    w_in_b = w_in[l].astype(BF16)

    proj, wg_b, wu_b, wd_b, wo_b = _norm_inproj(
        x2, attn_norm_w[l].reshape(1, d), col_scale, w_in_b, wq, w_gate[l], w_up[l], w_down[l], w_out[l],
        tm=512, tn=512, tf=512)
    oa = _diff_attention(
        proj, diff_slopes2,
        lambda_q1[l].reshape(1, HEAD_DIM), lambda_k1[l].reshape(1, HEAD_DIM),
        lambda_q2[l].reshape(1, HEAD_DIM), lambda_k2[l].reshape(1, HEAD_DIM),
        subln_w[l].reshape(1, 2 * HEAD_DIM),
        batch=batch, seq=seq, hp=1, tk=512, rc=256, lambda_init=lambda_init)
    ob = _swa_attention(proj, sinks[l], swa_tbl, batch=batch, seq=seq, tq=512)
    x1, h2 = _outproj(x2, oa, ob, wo_b, ffn_norm_w[l].reshape(1, d), tm=512, rc=256)
    out = _ffn(h2, x1, wg_b, wu_b, wd_b, final_norm_w.reshape(1, d), tm=512, tf=512, rc=256)
    return out.reshape(batch, seq, d)
```

```python
import functools
import math

import jax
import jax.numpy as jnp
import numpy as np
from jax import lax
from jax.experimental import pallas as pl
from jax.experimental.pallas import tpu as pltpu

D_MODEL = 2048
HEAD_DIM = 64
DIFF_HEADS = 8
DIFF_WIDTH = 1024
SWA_Q_HEADS = 16
SWA_KV_HEADS = 4
SWA_GROUP = 4
SWA_WIDTH = 1024
WINDOW = 128
D_FF = 5632
IN_COLS = 4608
RMS_EPS = 1e-5
LOG2E = math.log2(math.e)
Q_SCALE = HEAD_DIM ** -0.5 * LOG2E
NEG_INF = float("-inf")

V7X_VMEM_BYTES = 64 * 1024 * 1024
VMEM_LIMIT_BYTES = 56 * 1024 * 1024

F32 = jnp.float32
BF16 = jnp.bfloat16


def _alibi_slopes(n_heads):
    return np.array([2.0 ** (-8.0 * (h + 1) / n_heads) for h in range(n_heads)], dtype=np.float64)


def _rms_scale(x):
    return x * lax.rsqrt(jnp.mean(x * x, axis=-1, keepdims=True) + RMS_EPS)


def _norm_inproj_kernel(x_ref, nw_ref, cs_ref, wa_ref, wq_ref, wkv_ref, wg_ref, wu_ref, wd_ref, wo_ref,
                        o_ref, wg_o, wu_o, wd_o, wo_o, h_ref, *, tn):
    h_ref[...] = (_rms_scale(x_ref[...]) * nw_ref[...]).astype(BF16)
    col = 0
    for w_ref in (wa_ref, wq_ref, wkv_ref):
        for j in range(w_ref.shape[1] // tn):
            cols = slice(col, col + tn)
            acc = jnp.dot(h_ref[...], w_ref[:, j * tn:(j + 1) * tn], preferred_element_type=F32)
            o_ref[:, cols] = (acc * cs_ref[:, cols]).astype(o_ref.dtype)
            col += tn
    tf = wg_o.shape[2]
    for f in range(wg_o.shape[0]):
        wg_o[f] = wg_ref[:, f * tf:(f + 1) * tf].astype(BF16)
        wu_o[f] = wu_ref[:, f * tf:(f + 1) * tf].astype(BF16)
    wd_o[...] = wd_ref[...].astype(BF16)
    wo_o[...] = wo_ref[...].astype(BF16)


def _swa_out_row_block(i):
    n_diff = DIFF_WIDTH // HEAD_DIM
    t = i - n_diff
    per_pair = 2 * SWA_GROUP
    src = n_diff + (t // per_pair) * per_pair + (t % 2) * SWA_GROUP + (t % per_pair) // 2
    return jnp.where(i < n_diff, i, src)


def _norm_inproj(x2, norm_w, col_scale, w_in_b, wq_b, w_gate, w_up, w_down, w_out, *, tm, tn, tf):
    tokens, d = x2.shape
    n = w_in_b.shape[1]
    dff = w_gate.shape[1]
    steps = tokens // tm
    assert w_out.shape[0] == steps * HEAD_DIM and d % steps == 0 and dff % steps == 0
    n_a = 3 * DIFF_WIDTH
    resident = dict(pipeline_mode=pl.Buffered(1))
    rows = lambda r, c: pl.BlockSpec((r, c), lambda i: (i, 0))
    col_blocks = lambda: pl.BlockSpec((dff // tf, d // steps, tf), lambda i: (0, i, 0))
    return pl.pallas_call(
        functools.partial(_norm_inproj_kernel, tn=tn),
        out_shape=(
            jax.ShapeDtypeStruct((tokens, n), BF16),
            jax.ShapeDtypeStruct((dff // tf, d, tf), BF16),
            jax.ShapeDtypeStruct((dff // tf, d, tf), BF16),
            jax.ShapeDtypeStruct(w_down.shape, BF16),
            jax.ShapeDtypeStruct(w_out.shape, BF16),
        ),
        grid=(steps,),
        in_specs=[
            pl.BlockSpec((tm, d), lambda i: (i, 0)),
            pl.BlockSpec((1, d), lambda i: (0, 0), **resident),
            pl.BlockSpec((1, n), lambda i: (0, 0), **resident),
            pl.BlockSpec((d, n_a), lambda i: (0, 0), **resident),
            pl.BlockSpec((d, SWA_WIDTH), lambda i: (0, 0), **resident),
            pl.BlockSpec((d, n - n_a - SWA_WIDTH), lambda i: (0, (n_a + SWA_WIDTH) // (n - n_a - SWA_WIDTH)),
                         **resident),
            rows(d // steps, dff),
            rows(d // steps, dff),
            rows(dff // steps, d),
            pl.BlockSpec((HEAD_DIM, d), lambda i: (_swa_out_row_block(i), 0)),
        ],
        out_specs=(
            pl.BlockSpec((tm, n), lambda i: (i, 0)),
            col_blocks(),
            col_blocks(),
            rows(dff // steps, d),
            rows(HEAD_DIM, d),
        ),
        scratch_shapes=[pltpu.VMEM((tm, d), BF16)],
        compiler_params=pltpu.CompilerParams(
            dimension_semantics=("parallel",),
            vmem_limit_bytes=VMEM_LIMIT_BYTES,
        ),
        name="norm_inproj",
    )(x2, norm_w, col_scale, w_in_b, wq_b, w_in_b, w_gate, w_up, w_down, w_out)


def _diff_attn_kernel(slope_ref, lq1_ref, lk1_ref, lq2_ref, lk2_ref, q_ref, k_ref, v_ref, sw_ref,
                      o_ref, m_ref, acc_ref, *, hp, tk, rc, lambda_init):
    seq = q_ref.shape[0]
    lanes = 2 * HEAD_DIM
    n_blk = seq // tk
    nt = (((1,), (1,)), ((), ()))
    slope2 = [slope_ref[pl.program_id(1) * hp + hh] for hh in range(hp)]

    first_map = lax.broadcasted_iota(jnp.int32, (rc, lanes), 1) < HEAD_DIM
    kcol = lax.broadcasted_iota(jnp.int32, (1, tk), 1).astype(F32)
    ones = jnp.ones((tk, lanes), BF16)
    tri = lax.broadcasted_iota(jnp.int32, (rc, rc), 1) <= lax.broadcasted_iota(jnp.int32, (rc, rc), 0)
    lam = (jnp.exp(jnp.sum(lq1_ref[...] * lk1_ref[...], axis=-1, keepdims=True))
           - jnp.exp(jnp.sum(lq2_ref[...] * lk2_ref[...], axis=-1, keepdims=True))
           + lambda_init)

    steps = [(hh, i, j, sub, mp) for hh in range(hp) for j in range(n_blk) for i in range(j, n_blk)
             for sub in range(tk // rc) for mp in range(2)]

    def n_cols(i, j, sub):
        return sub * rc + rc if i == j else tk

    def logits(hh, i, j, sub, mp):
        k0, r0, ncol = j * tk, i * tk + sub * rc, n_cols(i, j, sub)
        hl = slice(hh * lanes, (hh + 1) * lanes)
        qc = q_ref[r0:r0 + rc, hl]
        qz = jnp.where(first_map if mp == 0 else ~first_map, qc, jnp.zeros_like(qc))
        bias = slope2[hh] * (kcol[:, :ncol] + float(k0 - i * tk))
        s = lax.dot_general(qz, k_ref[k0:k0 + ncol, hl], nt, preferred_element_type=F32) + bias
        if i == j:
            s_diag = jnp.where(tri, s[:, ncol - rc:], NEG_INF)
            s = jnp.concatenate([s[:, :ncol - rc], s_diag], axis=1) if ncol > rc else s_diag
        return s

    def update(hh, i, j, sub, mp, s, o1):
        k0, r0, ncol = j * tk, i * tk + sub * rc, n_cols(i, j, sub)
        hl = slice(hh * lanes, (hh + 1) * lanes)
        vaug = jnp.concatenate([v_ref[k0:k0 + ncol, hl], ones[:ncol]], axis=1)
        row0 = (2 * hh + mp) * seq + r0
        rows = slice(row0, row0 + rc)
        m_cur = jnp.max(s, axis=1, keepdims=True)
        if j == 0:
            m_new = jnp.broadcast_to(m_cur, (rc, lanes))
        else:
            m_prev = m_ref[rows, :]
            m_new = jnp.maximum(m_prev, m_cur)
        p = jnp.exp2(s - jnp.concatenate([m_new] * (ncol // lanes), axis=1))
        pv = jnp.dot(p.astype(BF16), vaug, preferred_element_type=F32)
        if j > 0:
            alpha = jnp.exp2(m_prev - m_new)
            pv = jnp.concatenate([alpha, alpha], axis=1) * acc_ref[rows, :] + pv
        if i > j:
            m_ref[rows, :] = m_new
            acc_ref[rows, :] = pv
            return None
        if mp == 0:
            return pv[:, :lanes] / pv[:, lanes:]
        o = o1 - lam * (pv[:, :lanes] / pv[:, lanes:])
        o = _rms_scale(o) * sw_ref[...] * (1.0 - lambda_init)
        o_ref[r0:r0 + rc, hl] = o.astype(o_ref.dtype)
        return None

    ahead = 2
    pending = [logits(*st) for st in steps[:ahead]]
    o1 = None
    for n, step in enumerate(steps):
        s_cur = pending.pop(0)
        if n + ahead < len(steps):
            pending.append(logits(*steps[n + ahead]))
        o1 = update(*step, s_cur, o1)


def _diff_attention(proj, slopes2, lq1, lk1, lq2, lk2, subln_w, *, batch, seq, hp, tk, rc, lambda_init):
    lane_blk = hp * 2 * HEAD_DIM
    k_blk0 = DIFF_WIDTH // lane_blk
    v_blk0 = 2 * DIFF_WIDTH // lane_blk
    vec = lambda: pl.BlockSpec((1, HEAD_DIM), lambda b, h: (0, 0))
    return pl.pallas_call(
        functools.partial(_diff_attn_kernel, hp=hp, tk=tk, rc=rc, lambda_init=lambda_init),
        out_shape=jax.ShapeDtypeStruct((batch * seq, DIFF_WIDTH), BF16),
        grid=(batch, DIFF_HEADS // hp),
        in_specs=[
            pl.BlockSpec(memory_space=pltpu.SMEM),
            vec(), vec(), vec(), vec(),
            pl.BlockSpec((seq, lane_blk), lambda b, h: (b, h)),
            pl.BlockSpec((seq, lane_blk), lambda b, h: (b, k_blk0 + h)),
            pl.BlockSpec((seq, lane_blk), lambda b, h: (b, v_blk0 + h)),
            pl.BlockSpec((1, 2 * HEAD_DIM), lambda b, h: (0, 0)),
        ],
        out_specs=pl.BlockSpec((seq, lane_blk), lambda b, h: (b, h)),
        scratch_shapes=[
            pltpu.VMEM((hp * 2 * seq, 2 * HEAD_DIM), F32),
            pltpu.VMEM((hp * 2 * seq, 4 * HEAD_DIM), F32),
        ],
        compiler_params=pltpu.CompilerParams(
            dimension_semantics=("parallel", "parallel"),
            vmem_limit_bytes=VMEM_LIMIT_BYTES,
        ),
        name="diff_attention",
    )(slopes2, lq1, lk1, lq2, lk2, proj, proj, proj, subln_w)


def _swa_tables():
    slopes = _alibi_slopes(SWA_Q_HEADS)
    i = np.arange(WINDOW)[:, None]
    j = np.arange(2 * WINDOW)[None, :]
    dist = WINDOW + i - j
    valid = (dist >= 0) & (dist < WINDOW)
    tbl = np.empty((SWA_KV_HEADS // 2, SWA_GROUP * 2 * WINDOW, 2 * WINDOW), np.float32)
    for pair in range(SWA_KV_HEADS // 2):
        for g in range(SWA_GROUP):
            for e in range(2):
                head = (2 * pair + e) * SWA_GROUP + g
                r0 = (g * 2 + e) * WINDOW
                tbl[pair, r0:r0 + WINDOW] = np.where(valid, -slopes[head] * LOG2E * dist, -np.inf)
    return tbl


def _swa_kernel(sink_ref, tbl_ref, q_ref, k_ref, v_ref, o_ref, *, tq):
    qi = pl.program_id(1)
    n_sub = tq // WINDOW
    lanes = 2 * HEAD_DIM
    lane_q = lax.broadcasted_iota(jnp.int32, (WINDOW, lanes), 1)
    lane_kv = lax.broadcasted_iota(jnp.int32, (2 * WINDOW, lanes), 1)
    ones_kv = jnp.ones((2 * WINDOW, lanes), BF16)
    kcol = lax.broadcasted_iota(jnp.int32, (1, 2 * WINDOW), 1)
    for sub in range(n_sub):
        r0 = sub * WINDOW
        blk = qi * n_sub + sub
        cur0 = pl.multiple_of(blk * WINDOW, WINDOW)
        prev0 = pl.multiple_of(jnp.maximum(blk - 1, 0) * WINDOW, WINDOW)
        for pair in range(SWA_KV_HEADS // 2):
            c0 = pair * 2 * HEAD_DIM
            k2 = jnp.concatenate([k_ref[pl.ds(prev0, WINDOW), c0:c0 + 2 * HEAD_DIM],
                                  k_ref[pl.ds(cur0, WINDOW), c0:c0 + 2 * HEAD_DIM]], axis=0)
            v2 = jnp.concatenate([v_ref[pl.ds(prev0, WINDOW), c0:c0 + 2 * HEAD_DIM],
                                  v_ref[pl.ds(cur0, WINDOW), c0:c0 + 2 * HEAD_DIM]], axis=0)
            zkv = jnp.zeros_like(v2)
            v_aug = [jnp.concatenate([jnp.where(lane_kv < HEAD_DIM, v2, zkv), ones_kv], axis=1),
                     jnp.concatenate([jnp.where(lane_kv >= HEAD_DIM, v2, zkv), ones_kv], axis=1)]
            rows = []
            for g in range(SWA_GROUP):
                t0 = (pair * SWA_GROUP + g) * 2 * HEAD_DIM
                qt = q_ref[r0:r0 + WINDOW, t0:t0 + 2 * HEAD_DIM]
                zq = jnp.zeros_like(qt)
                rows.append(jnp.where(lane_q < HEAD_DIM, qt, zq))
                rows.append(jnp.where(lane_q >= HEAD_DIM, qt, zq))
            q8 = jnp.concatenate(rows, axis=0)
            s = lax.dot_general(q8, k2, (((1,), (1,)), ((), ())), preferred_element_type=F32)
            for g in range(SWA_GROUP):
                o = None
                for e in range(2):
                    rb = (g * 2 + e) * WINDOW
                    sb = s[rb:rb + WINDOW] + tbl_ref[pair, rb:rb + WINDOW, :]
                    if sub == 0:
                        sb = sb + jnp.where((kcol < WINDOW) & (blk == 0), NEG_INF, 0.0)
                    sink = jnp.full((WINDOW, lanes), sink_ref[(2 * pair + e) * SWA_GROUP + g] * LOG2E, F32)
                    m = jnp.maximum(jnp.max(sb, axis=1, keepdims=True), sink)
                    p = jnp.exp2(sb - jnp.concatenate([m, m], axis=1))
                    pv = jnp.dot(p.astype(BF16), v_aug[e], preferred_element_type=F32)
                    part = pv[:, :lanes] / (pv[:, lanes:] + jnp.exp2(sink - m))
                    o = part if o is None else o + part
                t0 = (pair * SWA_GROUP + g) * 2 * HEAD_DIM
                o_ref[r0:r0 + WINDOW, t0:t0 + 2 * HEAD_DIM] = o.astype(o_ref.dtype)


def _swa_attention(proj, sinks, tbl, *, batch, seq, tq):
    nq = seq // tq
    q_blk = (3 * DIFF_WIDTH) // SWA_WIDTH
    kv_w = SWA_KV_HEADS * HEAD_DIM
    k_blk = (3 * DIFF_WIDTH + SWA_WIDTH) // kv_w
    return pl.pallas_call(
        functools.partial(_swa_kernel, tq=tq),
        out_shape=jax.ShapeDtypeStruct((batch * seq, SWA_WIDTH), BF16),
        grid=(batch, nq),
        in_specs=[
            pl.BlockSpec(memory_space=pltpu.SMEM),
            pl.BlockSpec(tbl.shape, lambda b, i: (0, 0, 0)),
            pl.BlockSpec((tq, SWA_WIDTH), lambda b, i: (b * nq + i, q_blk)),
            pl.BlockSpec((seq, kv_w), lambda b, i: (b, k_blk)),
            pl.BlockSpec((seq, kv_w), lambda b, i: (b, k_blk + 1)),
        ],
        out_specs=pl.BlockSpec((tq, SWA_WIDTH), lambda b, i: (b * nq + i, 0)),
        compiler_params=pltpu.CompilerParams(
            dimension_semantics=("parallel", "arbitrary"),
            vmem_limit_bytes=VMEM_LIMIT_BYTES,
        ),
        name="swa_attention",
    )(sinks, tbl, proj, proj, proj)


def _outproj_kernel(x_ref, oa_ref, ob_ref, wa_ref, wb_ref, nw_ref, x1_ref, h_ref, *, rc):
    for c in range(x_ref.shape[0] // rc):
        rows = slice(c * rc, (c + 1) * rc)
        y = jnp.dot(oa_ref[rows, :], wa_ref[...], preferred_element_type=F32)
        y = y + jnp.dot(ob_ref[rows, :], wb_ref[...], preferred_element_type=F32)
        x1 = x_ref[rows, :] + y
        x1_ref[rows, :] = x1
        h_ref[rows, :] = (_rms_scale(x1) * nw_ref[...]).astype(h_ref.dtype)


def _outproj(x2, oa, ob, w, norm_w, *, tm, rc):
    tokens, d = x2.shape
    resident = dict(pipeline_mode=pl.Buffered(1))
    return pl.pallas_call(
        functools.partial(_outproj_kernel, rc=rc),
        out_shape=(jax.ShapeDtypeStruct((tokens, d), F32), jax.ShapeDtypeStruct((tokens, d), BF16)),
        grid=(tokens // tm,),
        in_specs=[
            pl.BlockSpec((tm, d), lambda i: (i, 0)),
            pl.BlockSpec((tm, DIFF_WIDTH), lambda i: (i, 0)),
            pl.BlockSpec((tm, SWA_WIDTH), lambda i: (i, 0)),
            pl.BlockSpec((DIFF_WIDTH, d), lambda i: (0, 0), **resident),
            pl.BlockSpec((SWA_WIDTH, d), lambda i: (DIFF_WIDTH // SWA_WIDTH, 0), **resident),
            pl.BlockSpec((1, d), lambda i: (0, 0), **resident),
        ],
        out_specs=(pl.BlockSpec((tm, d), lambda i: (i, 0)), pl.BlockSpec((tm, d), lambda i: (i, 0))),
        compiler_params=pltpu.CompilerParams(
            dimension_semantics=("parallel",),
            vmem_limit_bytes=VMEM_LIMIT_BYTES,
        ),
        name="outproj_residual_norm",
    )(x2, oa, ob, w, w, norm_w)


def _ffn_kernel(h_ref, x1_ref, wg_ref, wu_ref, wd_ref, nw_ref, o_ref, *, rc):
    f = pl.program_id(1)

    @pl.when(f == 0)
    def _():
        o_ref[...] = x1_ref[...]

    def gated(c):
        h = h_ref[c * rc:(c + 1) * rc, :]
        g = jnp.dot(h, wg_ref[...], preferred_element_type=F32)
        u = jnp.dot(h, wu_ref[...], preferred_element_type=F32)
        return (g * (1.0 / (1.0 + jnp.exp(-g))) * u).astype(BF16)

    def accumulate(final):
        n_chunk = h_ref.shape[0] // rc
        a_next = gated(0)
        for c in range(n_chunk):
            a = a_next
            if c + 1 < n_chunk:
                a_next = gated(c + 1)
            rows = slice(c * rc, (c + 1) * rc)
            o = o_ref[rows, :] + jnp.dot(a, wd_ref[...], preferred_element_type=F32)
            o_ref[rows, :] = _rms_scale(o) * nw_ref[...] if final else o

    is_last = f == pl.num_programs(1) - 1
    pl.when(jnp.logical_not(is_last))(functools.partial(accumulate, False))
    pl.when(is_last)(functools.partial(accumulate, True))


def _ffn(h2, x1, wg, wu, wd, norm_w, *, tm, tf, rc):
    tokens, d = x1.shape
    dff = wd.shape[0]
    n_i, n_f = tokens // tm, dff // tf

    def x1_block(i, f):
        return (jnp.minimum(i + (f == n_f - 1).astype(jnp.int32), n_i - 1), 0)

    return pl.pallas_call(
        functools.partial(_ffn_kernel, rc=rc),
        out_shape=jax.ShapeDtypeStruct((tokens, d), F32),
        grid=(n_i, n_f),
        in_specs=[
            pl.BlockSpec((tm, d), lambda i, f: (i, 0)),
            pl.BlockSpec((tm, d), x1_block),
            pl.BlockSpec((None, d, tf), lambda i, f: (f, 0, 0)),
            pl.BlockSpec((None, d, tf), lambda i, f: (f, 0, 0)),
            pl.BlockSpec((tf, d), lambda i, f: (f, 0)),
            pl.BlockSpec((1, d), lambda i, f: (0, 0)),
        ],
        out_specs=pl.BlockSpec((tm, d), lambda i, f: (i, 0)),
        compiler_params=pltpu.CompilerParams(
            dimension_semantics=("parallel", "arbitrary"),
            vmem_limit_bytes=VMEM_LIMIT_BYTES,
        ),
        name="swiglu_ffn_final_norm",
    )(h2, x1, wg, wu, wd, norm_w)


def kernel(x, attn_norm_w, w_in, lambda_q1, lambda_k1, lambda_q2, lambda_k2, subln_w, sinks, w_out,
           ffn_norm_w, w_gate, w_up, w_down, final_norm_w):
    batch, seq, d = x.shape
    depth = w_in.shape[0]
    assert (d, w_in.shape[2], w_gate.shape[2]) == (D_MODEL, IN_COLS, D_FF)
    tokens = batch * seq
    x2 = x.reshape(tokens, d)

    q_cols = np.ones((1, IN_COLS), np.float32)
    q_cols[:, :DIFF_WIDTH] = Q_SCALE
    q_cols[:, 3 * DIFF_WIDTH:3 * DIFF_WIDTH + SWA_WIDTH] = Q_SCALE
    col_scale = jnp.asarray(q_cols)
    diff_slopes2 = jnp.asarray((_alibi_slopes(DIFF_HEADS) * LOG2E).astype(np.float32))
    swa_tbl = jnp.asarray(_swa_tables())
    n_pair = SWA_KV_HEADS // 2

    assert depth == 1
    l = 0
    lambda_init = 0.8 - 0.6 * math.exp(-0.3 * l)
    wq = w_in[l][:, 3 * DIFF_WIDTH:3 * DIFF_WIDTH + SWA_WIDTH].astype(BF16)
    wq = wq.reshape(d, n_pair, 2, SWA_GROUP, HEAD_DIM).transpose(0, 1, 3, 2, 4).reshape(d, SWA_WIDTH)
    w_in_b = w_in[l].astype(BF16)

    proj, wg_b, wu_b, wd_b, wo_b = _norm_inproj(
        x2, attn_norm_w[l].reshape(1, d), col_scale, w_in_b, wq, w_gate[l], w_up[l], w_down[l], w_out[l],
        tm=512, tn=512, tf=512)
    oa = _diff_attention(
        proj, diff_slopes2,
        lambda_q1[l].reshape(1, HEAD_DIM), lambda_k1[l].reshape(1, HEAD_DIM),
        lambda_q2[l].reshape(1, HEAD_DIM), lambda_k2[l].reshape(1, HEAD_DIM),
        subln_w[l].reshape(1, 2 * HEAD_DIM),
        batch=batch, seq=seq, hp=1, tk=512, rc=256, lambda_init=lambda_init)
    ob = _swa_attention(proj, sinks[l], swa_tbl, batch=batch, seq=seq, tq=512)
    x1, h2 = _outproj(x2, oa, ob, wo_b, ffn_norm_w[l].reshape(1, d), tm=512, rc=256)
    out = _ffn(h2, x1, wg_b, wu_b, wd_b, final_norm_w.reshape(1, d), tm=512, tf=512, rc=256)
    return out.reshape(batch, seq, d)
```

```python
import functools
import math

import jax
import jax.numpy as jnp
import numpy as np
from jax import lax
from jax.experimental import pallas as pl
from jax.experimental.pallas import tpu as pltpu

D_MODEL = 2048
HEAD_DIM = 64
DIFF_HEADS = 8
DIFF_WIDTH = 1024
SWA_Q_HEADS = 16
SWA_KV_HEADS = 4
SWA_GROUP = 4
SWA_WIDTH = 1024
WINDOW = 128
D_FF = 5632
IN_COLS = 4608
RMS_EPS = 1e-5
LOG2E = math.log2(math.e)
Q_SCALE = HEAD_DIM ** -0.5 * LOG2E
NEG_INF = float("-inf")

V7X_VMEM_BYTES = 64 * 1024 * 1024
VMEM_LIMIT_BYTES = 56 * 1024 * 1024

F32 = jnp.float32
BF16 = jnp.bfloat16


def _alibi_slopes(n_heads):
    return np.array([2.0 ** (-8.0 * (h + 1) / n_heads) for h in range(n_heads)], dtype=np.float64)


def _rms_scale(x):
    return x * lax.rsqrt(jnp.mean(x * x, axis=-1, keepdims=True) + RMS_EPS)


def _norm_inproj_kernel(x_ref, nw_ref, cs_ref, wa_ref, wq_ref, wkv_ref, wg_ref, wu_ref, wd_ref, wo_ref,
                        o_ref, wg_o, wu_o, wd_o, wo_o, h_ref, *, tn):
    h_ref[...] = (_rms_scale(x_ref[...]) * nw_ref[...]).astype(BF16)
    col = 0
    for w_ref in (wa_ref, wq_ref, wkv_ref):
        for j in range(w_ref.shape[1] // tn):
            cols = slice(col, col + tn)
            acc = jnp.dot(h_ref[...], w_ref[:, j * tn:(j + 1) * tn], preferred_element_type=F32)
            o_ref[:, cols] = (acc * cs_ref[:, cols]).astype(o_ref.dtype)
            col += tn
    tf = wg_o.shape[2]
    for f in range(wg_o.shape[0]):
        wg_o[f] = wg_ref[:, f * tf:(f + 1) * tf].astype(BF16)
        wu_o[f] = wu_ref[:, f * tf:(f + 1) * tf].astype(BF16)
    wd_o[...] = wd_ref[...].astype(BF16)
    wo_o[...] = wo_ref[...].astype(BF16)


def _swa_out_row_block(i):
    n_diff = DIFF_WIDTH // HEAD_DIM
    t = i - n_diff
    per_pair = 2 * SWA_GROUP
    src = n_diff + (t // per_pair) * per_pair + (t % 2) * SWA_GROUP + (t % per_pair) // 2
    return jnp.where(i < n_diff, i, src)


def _norm_inproj(x2, norm_w, col_scale, w_in_b, wq_b, w_gate, w_up, w_down, w_out, *, tm, tn, tf):
    tokens, d = x2.shape
    n = w_in_b.shape[1]
    dff = w_gate.shape[1]
    steps = tokens // tm
    assert w_out.shape[0] == steps * HEAD_DIM and d % steps == 0 and dff % steps == 0
    n_a = 3 * DIFF_WIDTH
    resident = dict(pipeline_mode=pl.Buffered(1))
    rows = lambda r, c: pl.BlockSpec((r, c), lambda i: (i, 0))
    col_blocks = lambda: pl.BlockSpec((dff // tf, d // steps, tf), lambda i: (0, i, 0))
    return pl.pallas_call(
        functools.partial(_norm_inproj_kernel, tn=tn),
        out_shape=(
            jax.ShapeDtypeStruct((tokens, n), BF16),
            jax.ShapeDtypeStruct((dff // tf, d, tf), BF16),
            jax.ShapeDtypeStruct((dff // tf, d, tf), BF16),
            jax.ShapeDtypeStruct(w_down.shape, BF16),
            jax.ShapeDtypeStruct(w_out.shape, BF16),
        ),
        grid=(steps,),
        in_specs=[
            pl.BlockSpec((tm, d), lambda i: (i, 0)),
            pl.BlockSpec((1, d), lambda i: (0, 0), **resident),
            pl.BlockSpec((1, n), lambda i: (0, 0), **resident),
            pl.BlockSpec((d, n_a), lambda i: (0, 0), **resident),
            pl.BlockSpec((d, SWA_WIDTH), lambda i: (0, 0), **resident),
            pl.BlockSpec((d, n - n_a - SWA_WIDTH), lambda i: (0, (n_a + SWA_WIDTH) // (n - n_a - SWA_WIDTH)),
                         **resident),
            rows(d // steps, dff),
            rows(d // steps, dff),
            rows(dff // steps, d),
            pl.BlockSpec((HEAD_DIM, d), lambda i: (_swa_out_row_block(i), 0)),
        ],
        out_specs=(
            pl.BlockSpec((tm, n), lambda i: (i, 0)),
            col_blocks(),
            col_blocks(),
            rows(dff // steps, d),
            rows(HEAD_DIM, d),
        ),
        scratch_shapes=[pltpu.VMEM((tm, d), BF16)],
        compiler_params=pltpu.CompilerParams(
            dimension_semantics=("parallel",),
            vmem_limit_bytes=VMEM_LIMIT_BYTES,
        ),
        name="norm_inproj",
    )(x2, norm_w, col_scale, w_in_b, wq_b, w_in_b, w_gate, w_up, w_down, w_out)


def _diff_attn_kernel(slope_ref, lq1_ref, lk1_ref, lq2_ref, lk2_ref, q_ref, k_ref, v_ref, sw_ref,
                      o_ref, m_ref, acc_ref, *, tk, rc, lambda_init):
    seq = q_ref.shape[0]
    lanes = 2 * HEAD_DIM
    n_blk = seq // tk
    nt = (((1,), (1,)), ((), ()))
    slope2 = slope_ref[pl.program_id(1)]

    first_map = lax.broadcasted_iota(jnp.int32, (rc, lanes), 1) < HEAD_DIM
    kcol = lax.broadcasted_iota(jnp.int32, (1, tk), 1).astype(F32)
    ones = jnp.ones((tk, lanes), BF16)
    tri = lax.broadcasted_iota(jnp.int32, (rc, rc), 1) <= lax.broadcasted_iota(jnp.int32, (rc, rc), 0)
    lam = (jnp.exp(jnp.sum(lq1_ref[...] * lk1_ref[...], axis=-1, keepdims=True))
           - jnp.exp(jnp.sum(lq2_ref[...] * lk2_ref[...], axis=-1, keepdims=True))
           + lambda_init)

    steps = [(i, j, sub, mp) for j in range(n_blk) for i in range(j, n_blk)
             for sub in range(tk // rc) for mp in range(2)]

    def n_cols(i, j, sub):
        return sub * rc + rc if i == j else tk

    def logits(i, j, sub, mp):
        k0, r0, ncol = j * tk, i * tk + sub * rc, n_cols(i, j, sub)
        qc = q_ref[r0:r0 + rc, :]
        qz = jnp.where(first_map if mp == 0 else ~first_map, qc, jnp.zeros_like(qc))
        bias = slope2 * (kcol[:, :ncol] + float(k0 - i * tk))
        s = lax.dot_general(qz, k_ref[k0:k0 + ncol, :], nt, preferred_element_type=F32) + bias
        if i == j:
            s_diag = jnp.where(tri, s[:, ncol - rc:], NEG_INF)
            s = jnp.concatenate([s[:, :ncol - rc], s_diag], axis=1) if ncol > rc else s_diag
        return s

    def update(i, j, sub, mp, s, o1):
        k0, r0, ncol = j * tk, i * tk + sub * rc, n_cols(i, j, sub)
        vaug = jnp.concatenate([v_ref[k0:k0 + ncol, :], ones[:ncol]], axis=1)
        row0 = mp * seq + r0
        rows = slice(row0, row0 + rc)
        m_cur = jnp.max(s, axis=1, keepdims=True)
        if j == 0:
            m_new = jnp.broadcast_to(m_cur, (rc, lanes))
        else:
            m_prev = m_ref[rows, :]
            m_new = jnp.maximum(m_prev, m_cur)
        p = jnp.exp2(s - jnp.concatenate([m_new] * (ncol // lanes), axis=1))
        pv = jnp.dot(p.astype(BF16), vaug, preferred_element_type=F32)
        if j > 0:
            alpha = jnp.exp2(m_prev - m_new)
            pv = jnp.concatenate([alpha, alpha], axis=1) * acc_ref[rows, :] + pv
        if i > j:
            m_ref[rows, :] = m_new
            acc_ref[rows, :] = pv
            return None
        if mp == 0:
            return pv[:, :lanes] / pv[:, lanes:]
        o = o1 - lam * (pv[:, :lanes] / pv[:, lanes:])
        o = _rms_scale(o) * sw_ref[...] * (1.0 - lambda_init)
        o_ref[r0:r0 + rc, :] = o.astype(o_ref.dtype)
        return None

    ahead = 2
    pending = [logits(*st) for st in steps[:ahead]]
    o1 = None
    for n, step in enumerate(steps):
        s_cur = pending.pop(0)
        if n + ahead < len(steps):
            pending.append(logits(*steps[n + ahead]))
        o1 = update(*step, s_cur, o1)


def _diff_attention(proj, slopes2, lq1, lk1, lq2, lk2, subln_w, *, batch, seq, tk, rc, lambda_init):
    lane_blk = 2 * HEAD_DIM
    k_blk0 = DIFF_WIDTH // lane_blk
    v_blk0 = 2 * DIFF_WIDTH // lane_blk
    vec = lambda: pl.BlockSpec((1, HEAD_DIM), lambda b, h: (0, 0))
    return pl.pallas_call(
        functools.partial(_diff_attn_kernel, tk=tk, rc=rc, lambda_init=lambda_init),
        out_shape=jax.ShapeDtypeStruct((batch * seq, DIFF_WIDTH), BF16),
        grid=(batch, DIFF_HEADS),
        in_specs=[
            pl.BlockSpec(memory_space=pltpu.SMEM),
            vec(), vec(), vec(), vec(),
            pl.BlockSpec((seq, lane_blk), lambda b, h: (b, h)),
            pl.BlockSpec((seq, lane_blk), lambda b, h: (b, k_blk0 + h)),
            pl.BlockSpec((seq, lane_blk), lambda b, h: (b, v_blk0 + h)),
            pl.BlockSpec((1, lane_blk), lambda b, h: (0, 0)),
        ],
        out_specs=pl.BlockSpec((seq, lane_blk), lambda b, h: (b, h)),
        scratch_shapes=[
            pltpu.VMEM((2 * seq, lane_blk), F32),
            pltpu.VMEM((2 * seq, 2 * lane_blk), F32),
        ],
        compiler_params=pltpu.CompilerParams(
            dimension_semantics=("parallel", "parallel"),
            vmem_limit_bytes=VMEM_LIMIT_BYTES,
        ),
        name="diff_attention",
    )(slopes2, lq1, lk1, lq2, lk2, proj, proj, proj, subln_w)


def _swa_tables():
    slopes = _alibi_slopes(SWA_Q_HEADS)
    i = np.arange(WINDOW)[:, None]
    j = np.arange(2 * WINDOW)[None, :]
    dist = WINDOW + i - j
    valid = (dist >= 0) & (dist < WINDOW)
    tbl = np.empty((SWA_KV_HEADS // 2, SWA_GROUP * 2 * WINDOW, 2 * WINDOW), np.float32)
    for pair in range(SWA_KV_HEADS // 2):
        for g in range(SWA_GROUP):
            for e in range(2):
                head = (2 * pair + e) * SWA_GROUP + g
                r0 = (g * 2 + e) * WINDOW
                tbl[pair, r0:r0 + WINDOW] = np.where(valid, -slopes[head] * LOG2E * dist, -np.inf)
    return tbl


def _swa_kernel(sink_ref, tbl_ref, q_ref, k_ref, v_ref, o_ref, *, tq):
    qi = pl.program_id(1)
    n_sub = tq // WINDOW
    lanes = 2 * HEAD_DIM
    lane_q = lax.broadcasted_iota(jnp.int32, (WINDOW, lanes), 1)
    lane_kv = lax.broadcasted_iota(jnp.int32, (2 * WINDOW, lanes), 1)
    ones_kv = jnp.ones((2 * WINDOW, lanes), BF16)
    kcol = lax.broadcasted_iota(jnp.int32, (1, 2 * WINDOW), 1)
    for sub in range(n_sub):
        r0 = sub * WINDOW
        blk = qi * n_sub + sub
        cur0 = pl.multiple_of(blk * WINDOW, WINDOW)
        prev0 = pl.multiple_of(jnp.maximum(blk - 1, 0) * WINDOW, WINDOW)
        for pair in range(SWA_KV_HEADS // 2):
            c0 = pair * 2 * HEAD_DIM
            k2 = jnp.concatenate([k_ref[pl.ds(prev0, WINDOW), c0:c0 + 2 * HEAD_DIM],
                                  k_ref[pl.ds(cur0, WINDOW), c0:c0 + 2 * HEAD_DIM]], axis=0)
            v2 = jnp.concatenate([v_ref[pl.ds(prev0, WINDOW), c0:c0 + 2 * HEAD_DIM],
                                  v_ref[pl.ds(cur0, WINDOW), c0:c0 + 2 * HEAD_DIM]], axis=0)
            zkv = jnp.zeros_like(v2)
            v_aug = [jnp.concatenate([jnp.where(lane_kv < HEAD_DIM, v2, zkv), ones_kv], axis=1),
                     jnp.concatenate([jnp.where(lane_kv >= HEAD_DIM, v2, zkv), ones_kv], axis=1)]
            rows = []
            for g in range(SWA_GROUP):
                t0 = (pair * SWA_GROUP + g) * 2 * HEAD_DIM
                qt = q_ref[r0:r0 + WINDOW, t0:t0 + 2 * HEAD_DIM]
                zq = jnp.zeros_like(qt)
                rows.append(jnp.where(lane_q < HEAD_DIM, qt, zq))
                rows.append(jnp.where(lane_q >= HEAD_DIM, qt, zq))
            q8 = jnp.concatenate(rows, axis=0)
            s = lax.dot_general(q8, k2, (((1,), (1,)), ((), ())), preferred_element_type=F32)
            for g in range(SWA_GROUP):
                o = None
                for e in range(2):
                    rb = (g * 2 + e) * WINDOW
                    sb = s[rb:rb + WINDOW] + tbl_ref[pair, rb:rb + WINDOW, :]
                    if sub == 0:
                        sb = sb + jnp.where((kcol < WINDOW) & (blk == 0), NEG_INF, 0.0)
                    sink = jnp.full((WINDOW, lanes), sink_ref[(2 * pair + e) * SWA_GROUP + g] * LOG2E, F32)
                    m = jnp.maximum(jnp.max(sb, axis=1, keepdims=True), sink)
                    p = jnp.exp2(sb - jnp.concatenate([m, m], axis=1))
                    pv = jnp.dot(p.astype(BF16), v_aug[e], preferred_element_type=F32)
                    part = pv[:, :lanes] / (pv[:, lanes:] + jnp.exp2(sink - m))
                    o = part if o is None else o + part
                t0 = (pair * SWA_GROUP + g) * 2 * HEAD_DIM
                o_ref[r0:r0 + WINDOW, t0:t0 + 2 * HEAD_DIM] = o.astype(o_ref.dtype)


def _swa_attention(proj, sinks, tbl, *, batch, seq, tq):
    nq = seq // tq
    q_blk = (3 * DIFF_WIDTH) // SWA_WIDTH
    kv_w = SWA_KV_HEADS * HEAD_DIM
    k_blk = (3 * DIFF_WIDTH + SWA_WIDTH) // kv_w
    return pl.pallas_call(
        functools.partial(_swa_kernel, tq=tq),
        out_shape=jax.ShapeDtypeStruct((batch * seq, SWA_WIDTH), BF16),
        grid=(batch, nq),
        in_specs=[
            pl.BlockSpec(memory_space=pltpu.SMEM),
            pl.BlockSpec(tbl.shape, lambda b, i: (0, 0, 0)),
            pl.BlockSpec((tq, SWA_WIDTH), lambda b, i: (b * nq + i, q_blk)),
            pl.BlockSpec((seq, kv_w), lambda b, i: (b, k_blk)),
            pl.BlockSpec((seq, kv_w), lambda b, i: (b, k_blk + 1)),
        ],
        out_specs=pl.BlockSpec((tq, SWA_WIDTH), lambda b, i: (b * nq + i, 0)),
        compiler_params=pltpu.CompilerParams(
            dimension_semantics=("parallel", "arbitrary"),
            vmem_limit_bytes=VMEM_LIMIT_BYTES,
        ),
        name="swa_attention",
    )(sinks, tbl, proj, proj, proj)


def _outproj_kernel(x_ref, oa_ref, ob_ref, wa_ref, wb_ref, nw_ref, x1_ref, h_ref, *, rc):
    for c in range(x_ref.shape[0] // rc):
        rows = slice(c * rc, (c + 1) * rc)
        y = jnp.dot(oa_ref[rows, :], wa_ref[...], preferred_element_type=F32)
        y = y + jnp.dot(ob_ref[rows, :], wb_ref[...], preferred_element_type=F32)
        x1 = x_ref[rows, :] + y
        x1_ref[rows, :] = x1
        h_ref[rows, :] = (_rms_scale(x1) * nw_ref[...]).astype(h_ref.dtype)


def _outproj(x2, oa, ob, w, norm_w, *, tm, rc):
    tokens, d = x2.shape
    resident = dict(pipeline_mode=pl.Buffered(1))
    return pl.pallas_call(
        functools.partial(_outproj_kernel, rc=rc),
        out_shape=(jax.ShapeDtypeStruct((tokens, d), F32), jax.ShapeDtypeStruct((tokens, d), BF16)),
        grid=(tokens // tm,),
        in_specs=[
            pl.BlockSpec((tm, d), lambda i: (i, 0)),
            pl.BlockSpec((tm, DIFF_WIDTH), lambda i: (i, 0)),
            pl.BlockSpec((tm, SWA_WIDTH), lambda i: (i, 0)),
            pl.BlockSpec((DIFF_WIDTH, d), lambda i: (0, 0), **resident),
            pl.BlockSpec((SWA_WIDTH, d), lambda i: (DIFF_WIDTH // SWA_WIDTH, 0), **resident),
            pl.BlockSpec((1, d), lambda i: (0, 0), **resident),
        ],
        out_specs=(pl.BlockSpec((tm, d), lambda i: (i, 0)), pl.BlockSpec((tm, d), lambda i: (i, 0))),
        compiler_params=pltpu.CompilerParams(
            dimension_semantics=("parallel",),
            vmem_limit_bytes=VMEM_LIMIT_BYTES,
        ),
        name="outproj_residual_norm",
    )(x2, oa, ob, w, w, norm_w)


def _ffn_kernel(h_ref, x1_ref, wg_ref, wu_ref, wd_ref, nw_ref, o_ref, *, rc):
    f = pl.program_id(1)

    @pl.when(f == 0)
    def _():
        o_ref[...] = x1_ref[...]

    def gated(c):
        h = h_ref[c * rc:(c + 1) * rc, :]
        g = jnp.dot(h, wg_ref[...], preferred_element_type=F32)
        u = jnp.dot(h, wu_ref[...], preferred_element_type=F32)
        return (g * (1.0 / (1.0 + jnp.exp(-g))) * u).astype(BF16)

    n_chunk = h_ref.shape[0] // rc
    a_next = gated(0)
    for c in range(n_chunk):
        a = a_next
        if c + 1 < n_chunk:
            a_next = gated(c + 1)
        o_ref[c * rc:(c + 1) * rc, :] += jnp.dot(a, wd_ref[...], preferred_element_type=F32)

    @pl.when(f == pl.num_programs(1) - 1)
    def _():
        o_ref[...] = _rms_scale(o_ref[...]) * nw_ref[...]


def _ffn(h2, x1, wg, wu, wd, norm_w, *, tm, tf, rc):
    tokens, d = x1.shape
    dff = wd.shape[0]
    return pl.pallas_call(
        functools.partial(_ffn_kernel, rc=rc),
        out_shape=jax.ShapeDtypeStruct((tokens, d), F32),
        grid=(tokens // tm, dff // tf),
        in_specs=[
            pl.BlockSpec((tm, d), lambda i, f: (i, 0)),
            pl.BlockSpec((tm, d), lambda i, f: (i, 0)),
            pl.BlockSpec((None, d, tf), lambda i, f: (f, 0, 0)),
            pl.BlockSpec((None, d, tf), lambda i, f: (f, 0, 0)),
            pl.BlockSpec((tf, d), lambda i, f: (f, 0)),
            pl.BlockSpec((1, d), lambda i, f: (0, 0)),
        ],
        out_specs=pl.BlockSpec((tm, d), lambda i, f: (i, 0)),
        compiler_params=pltpu.CompilerParams(
            dimension_semantics=("parallel", "arbitrary"),
            vmem_limit_bytes=VMEM_LIMIT_BYTES,
        ),
        name="swiglu_ffn_final_norm",
    )(h2, x1, wg, wu, wd, norm_w)


def kernel(x, attn_norm_w, w_in, lambda_q1, lambda_k1, lambda_q2, lambda_k2, subln_w, sinks, w_out,
           ffn_norm_w, w_gate, w_up, w_down, final_norm_w):
    batch, seq, d = x.shape
    depth = w_in.shape[0]
    assert (d, w_in.shape[2], w_gate.shape[2]) == (D_MODEL, IN_COLS, D_FF)
    tokens = batch * seq
    x2 = x.reshape(tokens, d)

    q_cols = np.ones((1, IN_COLS), np.float32)
    q_cols[:, :DIFF_WIDTH] = Q_SCALE
    q_cols[:, 3 * DIFF_WIDTH:3 * DIFF_WIDTH + SWA_WIDTH] = Q_SCALE
    col_scale = jnp.asarray(q_cols)
    diff_slopes2 = jnp.asarray((_alibi_slopes(DIFF_HEADS) * LOG2E).astype(np.float32))
    swa_tbl = jnp.asarray(_swa_tables())
    n_pair = SWA_KV_HEADS // 2

    assert depth == 1
    l = 0
    lambda_init = 0.8 - 0.6 * math.exp(-0.3 * l)
    wq = w_in[l][:, 3 * DIFF_WIDTH:3 * DIFF_WIDTH + SWA_WIDTH].astype(BF16)
    wq = wq.reshape(d, n_pair, 2, SWA_GROUP, HEAD_DIM).transpose(0, 1, 3, 2, 4).reshape(d, SWA_WIDTH)
    w_in_b = w_in[l].astype(BF16)

    proj, wg_b, wu_b, wd_b, wo_b = _norm_inproj(
        x2, attn_norm_w[l].reshape(1, d), col_scale, w_in_b, wq, w_gate[l], w_up[l], w_down[l], w_out[l],
        tm=512, tn=512, tf=512)
    oa = _diff_attention(
        proj, diff_slopes2,
        lambda_q1[l].reshape(1, HEAD_DIM), lambda_k1[l].reshape(1, HEAD_DIM),
        lambda_q2[l].reshape(1, HEAD_DIM), lambda_k2[l].reshape(1, HEAD_DIM),
        subln_w[l].reshape(1, 2 * HEAD_DIM),
        batch=batch, seq=seq, tk=512, rc=256, lambda_init=lambda_init)
    ob = _swa_attention(proj, sinks[l], swa_tbl, batch=batch, seq=seq, tq=512)
    x1, h2 = _outproj(x2, oa, ob, wo_b, ffn_norm_w[l].reshape(1, d), tm=512, rc=256)
    out = _ffn(h2, x1, wg_b, wu_b, wd_b, final_norm_w.reshape(1, d), tm=1024, tf=512, rc=256)
    return out.reshape(batch, seq, d)
```

```python
import functools
import math

import jax
import jax.numpy as jnp
import numpy as np
from jax import lax
from jax.experimental import pallas as pl
from jax.experimental.pallas import tpu as pltpu

D_MODEL = 2048
HEAD_DIM = 64
DIFF_HEADS = 8
DIFF_WIDTH = 1024
SWA_Q_HEADS = 16
SWA_KV_HEADS = 4
SWA_GROUP = 4
SWA_WIDTH = 1024
WINDOW = 128
D_FF = 5632
IN_COLS = 4608
RMS_EPS = 1e-5
LOG2E = math.log2(math.e)
Q_SCALE = HEAD_DIM ** -0.5 * LOG2E
NEG_INF = float("-inf")

V7X_VMEM_BYTES = 64 * 1024 * 1024
VMEM_LIMIT_BYTES = 56 * 1024 * 1024

F32 = jnp.float32
BF16 = jnp.bfloat16


def _alibi_slopes(n_heads):
    return np.array([2.0 ** (-8.0 * (h + 1) / n_heads) for h in range(n_heads)], dtype=np.float64)


def _rms_scale(x):
    return x * lax.rsqrt(jnp.mean(x * x, axis=-1, keepdims=True) + RMS_EPS)


def _norm_inproj_kernel(x_ref, nw_ref, cs_ref, wa_ref, wq_ref, wkv_ref, wg_ref, wu_ref, wd_ref, wo_ref,
                        o_ref, wg_o, wu_o, wd_o, wo_o, h_ref, *, tn):
    h_ref[...] = (_rms_scale(x_ref[...]) * nw_ref[...]).astype(BF16)
    col = 0
    for w_ref in (wa_ref, wq_ref, wkv_ref):
        for j in range(w_ref.shape[1] // tn):
            cols = slice(col, col + tn)
            acc = jnp.dot(h_ref[...], w_ref[:, j * tn:(j + 1) * tn], preferred_element_type=F32)
            o_ref[:, cols] = (acc * cs_ref[:, cols]).astype(o_ref.dtype)
            col += tn
    tf = wg_o.shape[2]
    for f in range(wg_o.shape[0]):
        wg_o[f] = wg_ref[:, f * tf:(f + 1) * tf].astype(BF16)
        wu_o[f] = wu_ref[:, f * tf:(f + 1) * tf].astype(BF16)
    wd_o[...] = wd_ref[...].astype(BF16)
    wo_o[...] = wo_ref[...].astype(BF16)


def _swa_out_row_block(i):
    n_diff = DIFF_WIDTH // HEAD_DIM
    t = i - n_diff
    per_pair = 2 * SWA_GROUP
    src = n_diff + (t // per_pair) * per_pair + (t % 2) * SWA_GROUP + (t % per_pair) // 2
    return jnp.where(i < n_diff, i, src)


def _norm_inproj(x2, norm_w, col_scale, w_in_b, wq_b, w_gate, w_up, w_down, w_out, *, tm, tn, tf):
    tokens, d = x2.shape
    n = w_in_b.shape[1]
    dff = w_gate.shape[1]
    steps = tokens // tm
    assert w_out.shape[0] == steps * HEAD_DIM and d % steps == 0 and dff % steps == 0
    n_a = 3 * DIFF_WIDTH
    resident = dict(pipeline_mode=pl.Buffered(1))
    rows = lambda r, c: pl.BlockSpec((r, c), lambda i: (i, 0))
    col_blocks = lambda: pl.BlockSpec((dff // tf, d // steps, tf), lambda i: (0, i, 0))
    return pl.pallas_call(
        functools.partial(_norm_inproj_kernel, tn=tn),
        out_shape=(
            jax.ShapeDtypeStruct((tokens, n), BF16),
            jax.ShapeDtypeStruct((dff // tf, d, tf), BF16),
            jax.ShapeDtypeStruct((dff // tf, d, tf), BF16),
            jax.ShapeDtypeStruct(w_down.shape, BF16),
            jax.ShapeDtypeStruct(w_out.shape, BF16),
        ),
        grid=(steps,),
        in_specs=[
            pl.BlockSpec((tm, d), lambda i: (i, 0)),
            pl.BlockSpec((1, d), lambda i: (0, 0), **resident),
            pl.BlockSpec((1, n), lambda i: (0, 0), **resident),
            pl.BlockSpec((d, n_a), lambda i: (0, 0), **resident),
            pl.BlockSpec((d, SWA_WIDTH), lambda i: (0, 0), **resident),
            pl.BlockSpec((d, n - n_a - SWA_WIDTH), lambda i: (0, (n_a + SWA_WIDTH) // (n - n_a - SWA_WIDTH)),
                         **resident),
            rows(d // steps, dff),
            rows(d // steps, dff),
            rows(dff // steps, d),
            pl.BlockSpec((HEAD_DIM, d), lambda i: (_swa_out_row_block(i), 0)),
        ],
        out_specs=(
            pl.BlockSpec((tm, n), lambda i: (i, 0)),
            col_blocks(),
            col_blocks(),
            rows(dff // steps, d),
            rows(HEAD_DIM, d),
        ),
        scratch_shapes=[pltpu.VMEM((tm, d), BF16)],
        compiler_params=pltpu.CompilerParams(
            dimension_semantics=("parallel",),
            vmem_limit_bytes=VMEM_LIMIT_BYTES,
        ),
        name="norm_inproj",
    )(x2, norm_w, col_scale, w_in_b, wq_b, w_in_b, w_gate, w_up, w_down, w_out)


def _diff_attn_kernel(slope_ref, lq1_ref, lk1_ref, lq2_ref, lk2_ref, q_ref, k_ref, v_ref, sw_ref,
                      o_ref, m_ref, acc_ref, *, tk, rc, lambda_init):
    seq = q_ref.shape[0]
    lanes = 2 * HEAD_DIM
    n_blk = seq // tk
    nt = (((1,), (1,)), ((), ()))
    slope2 = slope_ref[pl.program_id(1)]

    first_map = lax.broadcasted_iota(jnp.int32, (rc, lanes), 1) < HEAD_DIM
    kcol = lax.broadcasted_iota(jnp.int32, (1, tk), 1).astype(F32)
    ones = jnp.ones((tk, lanes), BF16)
    tri = lax.broadcasted_iota(jnp.int32, (rc, rc), 1) <= lax.broadcasted_iota(jnp.int32, (rc, rc), 0)
    lam = (jnp.exp(jnp.sum(lq1_ref[...] * lk1_ref[...], axis=-1, keepdims=True))
           - jnp.exp(jnp.sum(lq2_ref[...] * lk2_ref[...], axis=-1, keepdims=True))
           + lambda_init)

    steps = [(i, j, sub, mp) for j in range(n_blk) for i in range(j, n_blk)
             for sub in range(tk // rc) for mp in range(2)]
    steps = [st for st in steps if st[0] > 0] + [st for st in steps if st[0] == 0]
    map1_out = {}

    def n_cols(i, j, sub):
        return sub * rc + rc if i == j else tk

    def logits(i, j, sub, mp):
        k0, r0, ncol = j * tk, i * tk + sub * rc, n_cols(i, j, sub)
        qc = q_ref[r0:r0 + rc, :]
        qz = jnp.where(first_map if mp == 0 else ~first_map, qc, jnp.zeros_like(qc))
        bias = slope2 * (kcol[:, :ncol] + float(k0 - i * tk))
        s = lax.dot_general(qz, k_ref[k0:k0 + ncol, :], nt, preferred_element_type=F32) + bias
        if i == j:
            s_diag = jnp.where(tri, s[:, ncol - rc:], NEG_INF)
            s = jnp.concatenate([s[:, :ncol - rc], s_diag], axis=1) if ncol > rc else s_diag
        return s

    def update(i, j, sub, mp, s):
        k0, r0, ncol = j * tk, i * tk + sub * rc, n_cols(i, j, sub)
        vaug = jnp.concatenate([v_ref[k0:k0 + ncol, :], ones[:ncol]], axis=1)
        row0 = mp * seq + r0
        rows = slice(row0, row0 + rc)
        m_cur = jnp.max(s, axis=1, keepdims=True)
        if j == 0:
            m_new = jnp.broadcast_to(m_cur, (rc, lanes))
        else:
            m_prev = m_ref[rows, :]
            m_new = jnp.maximum(m_prev, m_cur)
        p = jnp.exp2(s - jnp.concatenate([m_new] * (ncol // lanes), axis=1))
        pv = jnp.dot(p.astype(BF16), vaug, preferred_element_type=F32)
        if j > 0:
            alpha = jnp.exp2(m_prev - m_new)
            pv = jnp.concatenate([alpha, alpha], axis=1) * acc_ref[rows, :] + pv
        if i > j:
            m_ref[rows, :] = m_new
            acc_ref[rows, :] = pv
        elif mp == 0:
            map1_out[r0] = pv[:, :lanes] / pv[:, lanes:]
        else:
            o = map1_out.pop(r0) - lam * (pv[:, :lanes] / pv[:, lanes:])
            o = _rms_scale(o) * sw_ref[...] * (1.0 - lambda_init)
            o_ref[r0:r0 + rc, :] = o.astype(o_ref.dtype)

    ahead = 2
    pending = [logits(*st) for st in steps[:ahead]]
    for n, step in enumerate(steps):
        s_cur = pending.pop(0)
        if n + ahead < len(steps):
            pending.append(logits(*steps[n + ahead]))
        update(*step, s_cur)


def _diff_attention(proj, slopes2, lq1, lk1, lq2, lk2, subln_w, *, batch, seq, tk, rc, lambda_init):
    lane_blk = 2 * HEAD_DIM
    k_blk0 = DIFF_WIDTH // lane_blk
    v_blk0 = 2 * DIFF_WIDTH // lane_blk
    vec = lambda: pl.BlockSpec((1, HEAD_DIM), lambda b, h: (0, 0))
    return pl.pallas_call(
        functools.partial(_diff_attn_kernel, tk=tk, rc=rc, lambda_init=lambda_init),
        out_shape=jax.ShapeDtypeStruct((batch * seq, DIFF_WIDTH), BF16),
        grid=(batch, DIFF_HEADS),
        in_specs=[
            pl.BlockSpec(memory_space=pltpu.SMEM),
            vec(), vec(), vec(), vec(),
            pl.BlockSpec((seq, lane_blk), lambda b, h: (b, h)),
            pl.BlockSpec((seq, lane_blk), lambda b, h: (b, k_blk0 + h)),
            pl.BlockSpec((seq, lane_blk), lambda b, h: (b, v_blk0 + h)),
            pl.BlockSpec((1, lane_blk), lambda b, h: (0, 0)),
        ],
        out_specs=pl.BlockSpec((seq, lane_blk), lambda b, h: (b, h)),
        scratch_shapes=[
            pltpu.VMEM((2 * seq, lane_blk), F32),
            pltpu.VMEM((2 * seq, 2 * lane_blk), F32),
        ],
        compiler_params=pltpu.CompilerParams(
            dimension_semantics=("parallel", "parallel"),
            vmem_limit_bytes=VMEM_LIMIT_BYTES,
        ),
        name="diff_attention",
    )(slopes2, lq1, lk1, lq2, lk2, proj, proj, proj, subln_w)


def _swa_tables():
    slopes = _alibi_slopes(SWA_Q_HEADS)
    i = np.arange(WINDOW)[:, None]
    j = np.arange(2 * WINDOW)[None, :]
    dist = WINDOW + i - j
    valid = (dist >= 0) & (dist < WINDOW)
    tbl = np.empty((SWA_KV_HEADS // 2, SWA_GROUP * 2 * WINDOW, 2 * WINDOW), np.float32)
    for pair in range(SWA_KV_HEADS // 2):
        for g in range(SWA_GROUP):
            for e in range(2):
                head = (2 * pair + e) * SWA_GROUP + g
                r0 = (g * 2 + e) * WINDOW
                tbl[pair, r0:r0 + WINDOW] = np.where(valid, -slopes[head] * LOG2E * dist, -np.inf)
    return tbl


def _swa_kernel(sink_ref, tbl_ref, q_ref, k_ref, v_ref, o_ref, *, tq):
    qi = pl.program_id(1)
    n_sub = tq // WINDOW
    lanes = 2 * HEAD_DIM
    lane_q = lax.broadcasted_iota(jnp.int32, (WINDOW, lanes), 1)
    lane_kv = lax.broadcasted_iota(jnp.int32, (2 * WINDOW, lanes), 1)
    ones_kv = jnp.ones((2 * WINDOW, lanes), BF16)
    kcol = lax.broadcasted_iota(jnp.int32, (1, 2 * WINDOW), 1)
    for sub in range(n_sub):
        r0 = sub * WINDOW
        blk = qi * n_sub + sub
        cur0 = pl.multiple_of(blk * WINDOW, WINDOW)
        prev0 = pl.multiple_of(jnp.maximum(blk - 1, 0) * WINDOW, WINDOW)
        for pair in range(SWA_KV_HEADS // 2):
            c0 = pair * 2 * HEAD_DIM
            k2 = jnp.concatenate([k_ref[pl.ds(prev0, WINDOW), c0:c0 + 2 * HEAD_DIM],
                                  k_ref[pl.ds(cur0, WINDOW), c0:c0 + 2 * HEAD_DIM]], axis=0)
            v2 = jnp.concatenate([v_ref[pl.ds(prev0, WINDOW), c0:c0 + 2 * HEAD_DIM],
                                  v_ref[pl.ds(cur0, WINDOW), c0:c0 + 2 * HEAD_DIM]], axis=0)
            zkv = jnp.zeros_like(v2)
            v_aug = [jnp.concatenate([jnp.where(lane_kv < HEAD_DIM, v2, zkv), ones_kv], axis=1),
                     jnp.concatenate([jnp.where(lane_kv >= HEAD_DIM, v2, zkv), ones_kv], axis=1)]
            rows = []
            for g in range(SWA_GROUP):
                t0 = (pair * SWA_GROUP + g) * 2 * HEAD_DIM
                qt = q_ref[r0:r0 + WINDOW, t0:t0 + 2 * HEAD_DIM]
                zq = jnp.zeros_like(qt)
                rows.append(jnp.where(lane_q < HEAD_DIM, qt, zq))
                rows.append(jnp.where(lane_q >= HEAD_DIM, qt, zq))
            q8 = jnp.concatenate(rows, axis=0)
            s = lax.dot_general(q8, k2, (((1,), (1,)), ((), ())), preferred_element_type=F32)
            for g in range(SWA_GROUP):
                o = None
                for e in range(2):
                    rb = (g * 2 + e) * WINDOW
                    sb = s[rb:rb + WINDOW] + tbl_ref[pair, rb:rb + WINDOW, :]
                    if sub == 0:
                        sb = sb + jnp.where((kcol < WINDOW) & (blk == 0), NEG_INF, 0.0)
                    sink = jnp.full((WINDOW, lanes), sink_ref[(2 * pair + e) * SWA_GROUP + g] * LOG2E, F32)
                    m = jnp.maximum(jnp.max(sb, axis=1, keepdims=True), sink)
                    p = jnp.exp2(sb - jnp.concatenate([m, m], axis=1))
                    pv = jnp.dot(p.astype(BF16), v_aug[e], preferred_element_type=F32)
                    part = pv[:, :lanes] / (pv[:, lanes:] + jnp.exp2(sink - m))
                    o = part if o is None else o + part
                t0 = (pair * SWA_GROUP + g) * 2 * HEAD_DIM
                o_ref[r0:r0 + WINDOW, t0:t0 + 2 * HEAD_DIM] = o.astype(o_ref.dtype)


def _swa_attention(proj, sinks, tbl, *, batch, seq, tq):
    nq = seq // tq
    q_blk = (3 * DIFF_WIDTH) // SWA_WIDTH
    kv_w = SWA_KV_HEADS * HEAD_DIM
    k_blk = (3 * DIFF_WIDTH + SWA_WIDTH) // kv_w
    return pl.pallas_call(
        functools.partial(_swa_kernel, tq=tq),
        out_shape=jax.ShapeDtypeStruct((batch * seq, SWA_WIDTH), BF16),
        grid=(batch, nq),
        in_specs=[
            pl.BlockSpec(memory_space=pltpu.SMEM),
            pl.BlockSpec(tbl.shape, lambda b, i: (0, 0, 0)),
            pl.BlockSpec((tq, SWA_WIDTH), lambda b, i: (b * nq + i, q_blk)),
            pl.BlockSpec((seq, kv_w), lambda b, i: (b, k_blk)),
            pl.BlockSpec((seq, kv_w), lambda b, i: (b, k_blk + 1)),
        ],
        out_specs=pl.BlockSpec((tq, SWA_WIDTH), lambda b, i: (b * nq + i, 0)),
        compiler_params=pltpu.CompilerParams(
            dimension_semantics=("parallel", "arbitrary"),
            vmem_limit_bytes=VMEM_LIMIT_BYTES,
        ),
        name="swa_attention",
    )(sinks, tbl, proj, proj, proj)


def _outproj_kernel(x_ref, oa_ref, ob_ref, wa_ref, wb_ref, nw_ref, x1_ref, h_ref, *, rc):
    for c in range(x_ref.shape[0] // rc):
        rows = slice(c * rc, (c + 1) * rc)
        y = jnp.dot(oa_ref[rows, :], wa_ref[...], preferred_element_type=F32)
        y = y + jnp.dot(ob_ref[rows, :], wb_ref[...], preferred_element_type=F32)
        x1 = x_ref[rows, :] + y
        x1_ref[rows, :] = x1
        h_ref[rows, :] = (_rms_scale(x1) * nw_ref[...]).astype(h_ref.dtype)


def _outproj(x2, oa, ob, w, norm_w, *, tm, rc):
    tokens, d = x2.shape
    resident = dict(pipeline_mode=pl.Buffered(1))
    return pl.pallas_call(
        functools.partial(_outproj_kernel, rc=rc),
        out_shape=(jax.ShapeDtypeStruct((tokens, d), F32), jax.ShapeDtypeStruct((tokens, d), BF16)),
        grid=(tokens // tm,),
        in_specs=[
            pl.BlockSpec((tm, d), lambda i: (i, 0)),
            pl.BlockSpec((tm, DIFF_WIDTH), lambda i: (i, 0)),
            pl.BlockSpec((tm, SWA_WIDTH), lambda i: (i, 0)),
            pl.BlockSpec((DIFF_WIDTH, d), lambda i: (0, 0), **resident),
            pl.BlockSpec((SWA_WIDTH, d), lambda i: (DIFF_WIDTH // SWA_WIDTH, 0), **resident),
            pl.BlockSpec((1, d), lambda i: (0, 0), **resident),
        ],
        out_specs=(pl.BlockSpec((tm, d), lambda i: (i, 0)), pl.BlockSpec((tm, d), lambda i: (i, 0))),
        compiler_params=pltpu.CompilerParams(
            dimension_semantics=("parallel",),
            vmem_limit_bytes=VMEM_LIMIT_BYTES,
        ),
        name="outproj_residual_norm",
    )(x2, oa, ob, w, w, norm_w)


def _ffn_kernel(h_ref, x1_ref, wg_ref, wu_ref, wd_ref, nw_ref, o_ref, *, rc):
    f = pl.program_id(1)

    @pl.when(f == 0)
    def _():
        o_ref[...] = x1_ref[...]

    def gated(c):
        h = h_ref[c * rc:(c + 1) * rc, :]
        g = jnp.dot(h, wg_ref[...], preferred_element_type=F32)
        u = jnp.dot(h, wu_ref[...], preferred_element_type=F32)
        return (g * (1.0 / (1.0 + jnp.exp(-g))) * u).astype(BF16)

    n_chunk = h_ref.shape[0] // rc
    a_next = gated(0)
    for c in range(n_chunk):
        a = a_next
        if c + 1 < n_chunk:
            a_next = gated(c + 1)
        o_ref[c * rc:(c + 1) * rc, :] += jnp.dot(a, wd_ref[...], preferred_element_type=F32)

    @pl.when(f == pl.num_programs(1) - 1)
    def _():
        o_ref[...] = _rms_scale(o_ref[...]) * nw_ref[...]


def _ffn(h2, x1, wg, wu, wd, norm_w, *, tm, tf, rc):
    tokens, d = x1.shape
    dff = wd.shape[0]
    return pl.pallas_call(
        functools.partial(_ffn_kernel, rc=rc),
        out_shape=jax.ShapeDtypeStruct((tokens, d), F32),
        grid=(tokens // tm, dff // tf),
        in_specs=[
            pl.BlockSpec((tm, d), lambda i, f: (i, 0)),
            pl.BlockSpec((tm, d), lambda i, f: (i, 0)),
            pl.BlockSpec((None, d, tf), lambda i, f: (f, 0, 0)),
            pl.BlockSpec((None, d, tf), lambda i, f: (f, 0, 0)),
            pl.BlockSpec((tf, d), lambda i, f: (f, 0)),
            pl.BlockSpec((1, d), lambda i, f: (0, 0)),
        ],
        out_specs=pl.BlockSpec((tm, d), lambda i, f: (i, 0)),
        compiler_params=pltpu.CompilerParams(
            dimension_semantics=("parallel", "arbitrary"),
            vmem_limit_bytes=VMEM_LIMIT_BYTES,
        ),
        name="swiglu_ffn_final_norm",
    )(h2, x1, wg, wu, wd, norm_w)


def kernel(x, attn_norm_w, w_in, lambda_q1, lambda_k1, lambda_q2, lambda_k2, subln_w, sinks, w_out,
           ffn_norm_w, w_gate, w_up, w_down, final_norm_w):
    batch, seq, d = x.shape
    depth = w_in.shape[0]
    assert (d, w_in.shape[2], w_gate.shape[2]) == (D_MODEL, IN_COLS, D_FF)
    tokens = batch * seq
    x2 = x.reshape(tokens, d)

    q_cols = np.ones((1, IN_COLS), np.float32)
    q_cols[:, :DIFF_WIDTH] = Q_SCALE
    q_cols[:, 3 * DIFF_WIDTH:3 * DIFF_WIDTH + SWA_WIDTH] = Q_SCALE
    col_scale = jnp.asarray(q_cols)
    diff_slopes2 = jnp.asarray((_alibi_slopes(DIFF_HEADS) * LOG2E).astype(np.float32))
    swa_tbl = jnp.asarray(_swa_tables())
    n_pair = SWA_KV_HEADS // 2

    assert depth == 1
    l = 0
    lambda_init = 0.8 - 0.6 * math.exp(-0.3 * l)
    wq = w_in[l][:, 3 * DIFF_WIDTH:3 * DIFF_WIDTH + SWA_WIDTH].astype(BF16)
    wq = wq.reshape(d, n_pair, 2, SWA_GROUP, HEAD_DIM).transpose(0, 1, 3, 2, 4).reshape(d, SWA_WIDTH)
    w_in_b = w_in[l].astype(BF16)

    proj, wg_b, wu_b, wd_b, wo_b = _norm_inproj(
        x2, attn_norm_w[l].reshape(1, d), col_scale, w_in_b, wq, w_gate[l], w_up[l], w_down[l], w_out[l],
        tm=512, tn=512, tf=512)
    oa = _diff_attention(
        proj, diff_slopes2,
        lambda_q1[l].reshape(1, HEAD_DIM), lambda_k1[l].reshape(1, HEAD_DIM),
        lambda_q2[l].reshape(1, HEAD_DIM), lambda_k2[l].reshape(1, HEAD_DIM),
        subln_w[l].reshape(1, 2 * HEAD_DIM),
        batch=batch, seq=seq, tk=512, rc=256, lambda_init=lambda_init)
    ob = _swa_attention(proj, sinks[l], swa_tbl, batch=batch, seq=seq, tq=512)
    x1, h2 = _outproj(x2, oa, ob, wo_b, ffn_norm_w[l].reshape(1, d), tm=512, rc=256)
    out = _ffn(h2, x1, wg_b, wu_b, wd_b, final_norm_w.reshape(1, d), tm=1024, tf=512, rc=256)
    return out.reshape(batch, seq, d)
```

```python
import functools
import math

import jax
import jax.numpy as jnp
import numpy as np
from jax import lax
from jax.experimental import pallas as pl
from jax.experimental.pallas import tpu as pltpu

D_MODEL = 2048
HEAD_DIM = 64
DIFF_HEADS = 8
DIFF_WIDTH = 1024
SWA_Q_HEADS = 16
SWA_KV_HEADS = 4
SWA_GROUP = 4
SWA_WIDTH = 1024
WINDOW = 128
D_FF = 5632
IN_COLS = 4608
RMS_EPS = 1e-5
LOG2E = math.log2(math.e)
Q_SCALE = HEAD_DIM ** -0.5 * LOG2E
NEG_INF = float("-inf")

V7X_VMEM_BYTES = 64 * 1024 * 1024
VMEM_LIMIT_BYTES = 56 * 1024 * 1024

F32 = jnp.float32
BF16 = jnp.bfloat16


def _alibi_slopes(n_heads):
    return np.array([2.0 ** (-8.0 * (h + 1) / n_heads) for h in range(n_heads)], dtype=np.float64)


def _rms_scale(x):
    return x * lax.rsqrt(jnp.mean(x * x, axis=-1, keepdims=True) + RMS_EPS)


def _swa_query_reorder(acc):
    pieces = [acc[:, (e * SWA_GROUP + g) * HEAD_DIM:(e * SWA_GROUP + g + 1) * HEAD_DIM]
              for g in range(SWA_GROUP) for e in range(2)]
    return jnp.concatenate(pieces, axis=1)


def _norm_inproj_kernel(x_ref, nw_ref, cs_ref, w_ref, wg_ref, wu_ref, wd_ref, wo_ref,
                        o_ref, wg_o, wu_o, wd_o, wo_o, h_ref, *, tn):
    h_ref[...] = (_rms_scale(x_ref[...]) * nw_ref[...]).astype(BF16)
    for j in range(w_ref.shape[1] // tn):
        cols = slice(j * tn, (j + 1) * tn)
        acc = jnp.dot(h_ref[...], w_ref[:, cols], preferred_element_type=F32)
        if 3 * DIFF_WIDTH <= j * tn < 3 * DIFF_WIDTH + SWA_WIDTH:
            acc = _swa_query_reorder(acc)
        o_ref[:, cols] = (acc * cs_ref[:, cols]).astype(o_ref.dtype)
    tf = wg_o.shape[2]
    for f in range(wg_o.shape[0]):
        wg_o[f] = wg_ref[:, f * tf:(f + 1) * tf].astype(BF16)
        wu_o[f] = wu_ref[:, f * tf:(f + 1) * tf].astype(BF16)
    wd_o[...] = wd_ref[...].astype(BF16)
    wo_o[...] = wo_ref[...].astype(BF16)


def _swa_out_row_block(i):
    n_diff = DIFF_WIDTH // HEAD_DIM
    t = i - n_diff
    per_pair = 2 * SWA_GROUP
    src = n_diff + (t // per_pair) * per_pair + (t % 2) * SWA_GROUP + (t % per_pair) // 2
    return jnp.where(i < n_diff, i, src)


def _norm_inproj(x2, norm_w, col_scale, w_in_b, w_gate, w_up, w_down, w_out, *, tm, tn, tf):
    tokens, d = x2.shape
    n = w_in_b.shape[1]
    dff = w_gate.shape[1]
    steps = tokens // tm
    assert w_out.shape[0] == steps * HEAD_DIM and d % steps == 0 and dff % steps == 0
    assert tn == 2 * SWA_GROUP * HEAD_DIM
    resident = dict(pipeline_mode=pl.Buffered(1))
    rows = lambda r, c: pl.BlockSpec((r, c), lambda i: (i, 0))
    col_blocks = lambda: pl.BlockSpec((dff // tf, d // steps, tf), lambda i: (0, i, 0))
    return pl.pallas_call(
        functools.partial(_norm_inproj_kernel, tn=tn),
        out_shape=(
            jax.ShapeDtypeStruct((tokens, n), BF16),
            jax.ShapeDtypeStruct((dff // tf, d, tf), BF16),
            jax.ShapeDtypeStruct((dff // tf, d, tf), BF16),
            jax.ShapeDtypeStruct(w_down.shape, BF16),
            jax.ShapeDtypeStruct(w_out.shape, BF16),
        ),
        grid=(steps,),
        in_specs=[
            pl.BlockSpec((tm, d), lambda i: (i, 0)),
            pl.BlockSpec((1, d), lambda i: (0, 0), **resident),
            pl.BlockSpec((1, n), lambda i: (0, 0), **resident),
            pl.BlockSpec((d, n), lambda i: (0, 0), **resident),
            rows(d // steps, dff),
            rows(d // steps, dff),
            rows(dff // steps, d),
            pl.BlockSpec((HEAD_DIM, d), lambda i: (_swa_out_row_block(i), 0)),
        ],
        out_specs=(
            pl.BlockSpec((tm, n), lambda i: (i, 0)),
            col_blocks(),
            col_blocks(),
            rows(dff // steps, d),
            rows(HEAD_DIM, d),
        ),
        scratch_shapes=[pltpu.VMEM((tm, d), BF16)],
        compiler_params=pltpu.CompilerParams(
            dimension_semantics=("parallel",),
            vmem_limit_bytes=VMEM_LIMIT_BYTES,
        ),
        name="norm_inproj",
    )(x2, norm_w, col_scale, w_in_b, w_gate, w_up, w_down, w_out)


def _diff_attn_kernel(slope_ref, lq1_ref, lk1_ref, lq2_ref, lk2_ref, q_ref, k_ref, v_ref, sw_ref,
                      o_ref, m_ref, acc_ref, *, tk, rc, lambda_init):
    seq = q_ref.shape[0]
    lanes = 2 * HEAD_DIM
    n_blk = seq // tk
    nt = (((1,), (1,)), ((), ()))
    slope2 = slope_ref[pl.program_id(1)]

    first_map = lax.broadcasted_iota(jnp.int32, (rc, lanes), 1) < HEAD_DIM
    kcol = lax.broadcasted_iota(jnp.int32, (1, tk), 1).astype(F32)
    ones = jnp.ones((tk, lanes), BF16)
    tri = lax.broadcasted_iota(jnp.int32, (rc, rc), 1) <= lax.broadcasted_iota(jnp.int32, (rc, rc), 0)
    lam = (jnp.exp(jnp.sum(lq1_ref[...] * lk1_ref[...], axis=-1, keepdims=True))
           - jnp.exp(jnp.sum(lq2_ref[...] * lk2_ref[...], axis=-1, keepdims=True))
           + lambda_init)

    steps = [(i, j, sub, mp) for j in range(n_blk) for i in range(j, n_blk)
             for sub in range(tk // rc) for mp in range(2)]
    steps = [st for st in steps if st[0] > 0] + [st for st in steps if st[0] == 0]
    map1_out = {}

    def n_cols(i, j, sub):
        return sub * rc + rc if i == j else tk

    def logits(i, j, sub, mp):
        k0, r0, ncol = j * tk, i * tk + sub * rc, n_cols(i, j, sub)
        qc = q_ref[r0:r0 + rc, :]
        qz = jnp.where(first_map if mp == 0 else ~first_map, qc, jnp.zeros_like(qc))
        bias = slope2 * (kcol[:, :ncol] + float(k0 - i * tk))
        s = lax.dot_general(qz, k_ref[k0:k0 + ncol, :], nt, preferred_element_type=F32) + bias
        if i == j:
            s_diag = jnp.where(tri, s[:, ncol - rc:], NEG_INF)
            s = jnp.concatenate([s[:, :ncol - rc], s_diag], axis=1) if ncol > rc else s_diag
        return s

    def update(i, j, sub, mp, s):
        k0, r0, ncol = j * tk, i * tk + sub * rc, n_cols(i, j, sub)
        vaug = jnp.concatenate([v_ref[k0:k0 + ncol, :], ones[:ncol]], axis=1)
        row0 = mp * seq + r0
        rows = slice(row0, row0 + rc)
        m_cur = jnp.max(s, axis=1, keepdims=True)
        if j == 0:
            m_new = jnp.broadcast_to(m_cur, (rc, lanes))
        else:
            m_prev = m_ref[rows, :]
            m_new = jnp.maximum(m_prev, m_cur)
        p = jnp.exp2(s - jnp.concatenate([m_new] * (ncol // lanes), axis=1))
        pv = jnp.dot(p.astype(BF16), vaug, preferred_element_type=F32)
        if j > 0:
            alpha = jnp.exp2(m_prev - m_new)
            pv = jnp.concatenate([alpha, alpha], axis=1) * acc_ref[rows, :] + pv
        if i > j:
            m_ref[rows, :] = m_new
            acc_ref[rows, :] = pv
        elif mp == 0:
            map1_out[r0] = pv[:, :lanes] / pv[:, lanes:]
        else:
            o = map1_out.pop(r0) - lam * (pv[:, :lanes] / pv[:, lanes:])
            o = _rms_scale(o) * sw_ref[...] * (1.0 - lambda_init)
            o_ref[r0:r0 + rc, :] = o.astype(o_ref.dtype)

    ahead = 2
    pending = [logits(*st) for st in steps[:ahead]]
    for n, step in enumerate(steps):
        s_cur = pending.pop(0)
        if n + ahead < len(steps):
            pending.append(logits(*steps[n + ahead]))
        update(*step, s_cur)


def _diff_attention(proj, slopes2, lq1, lk1, lq2, lk2, subln_w, *, batch, seq, tk, rc, lambda_init):
    lane_blk = 2 * HEAD_DIM
    k_blk0 = DIFF_WIDTH // lane_blk
    v_blk0 = 2 * DIFF_WIDTH // lane_blk
    vec = lambda: pl.BlockSpec((1, HEAD_DIM), lambda b, h: (0, 0))
    return pl.pallas_call(
        functools.partial(_diff_attn_kernel, tk=tk, rc=rc, lambda_init=lambda_init),
        out_shape=jax.ShapeDtypeStruct((batch * seq, DIFF_WIDTH), BF16),
        grid=(batch, DIFF_HEADS),
        in_specs=[
            pl.BlockSpec(memory_space=pltpu.SMEM),
            vec(), vec(), vec(), vec(),
            pl.BlockSpec((seq, lane_blk), lambda b, h: (b, h)),
            pl.BlockSpec((seq, lane_blk), lambda b, h: (b, k_blk0 + h)),
            pl.BlockSpec((seq, lane_blk), lambda b, h: (b, v_blk0 + h)),
            pl.BlockSpec((1, lane_blk), lambda b, h: (0, 0)),
        ],
        out_specs=pl.BlockSpec((seq, lane_blk), lambda b, h: (b, h)),
        scratch_shapes=[
            pltpu.VMEM((2 * seq, lane_blk), F32),
            pltpu.VMEM((2 * seq, 2 * lane_blk), F32),
        ],
        compiler_params=pltpu.CompilerParams(
            dimension_semantics=("parallel", "parallel"),
            vmem_limit_bytes=VMEM_LIMIT_BYTES,
        ),
        name="diff_attention",
    )(slopes2, lq1, lk1, lq2, lk2, proj, proj, proj, subln_w)


def _swa_tables():
    slopes = _alibi_slopes(SWA_Q_HEADS)
    i = np.arange(WINDOW)[:, None]
    j = np.arange(2 * WINDOW)[None, :]
    dist = WINDOW + i - j
    valid = (dist >= 0) & (dist < WINDOW)
    tbl = np.empty((SWA_KV_HEADS // 2, SWA_GROUP * 2 * WINDOW, 2 * WINDOW), np.float32)
    for pair in range(SWA_KV_HEADS // 2):
        for g in range(SWA_GROUP):
            for e in range(2):
                head = (2 * pair + e) * SWA_GROUP + g
                r0 = (g * 2 + e) * WINDOW
                tbl[pair, r0:r0 + WINDOW] = np.where(valid, -slopes[head] * LOG2E * dist, -np.inf)
    return tbl


def _swa_kernel(sink_ref, tbl_ref, q_ref, k_ref, v_ref, o_ref, *, tq):
    qi = pl.program_id(1)
    n_sub = tq // WINDOW
    lanes = 2 * HEAD_DIM
    lane_q = lax.broadcasted_iota(jnp.int32, (WINDOW, lanes), 1)
    lane_kv = lax.broadcasted_iota(jnp.int32, (2 * WINDOW, lanes), 1)
    ones_kv = jnp.ones((2 * WINDOW, lanes), BF16)
    kcol = lax.broadcasted_iota(jnp.int32, (1, 2 * WINDOW), 1)
    for sub in range(n_sub):
        r0 = sub * WINDOW
        blk = qi * n_sub + sub
        cur0 = pl.multiple_of(blk * WINDOW, WINDOW)
        prev0 = pl.multiple_of(jnp.maximum(blk - 1, 0) * WINDOW, WINDOW)
        for pair in range(SWA_KV_HEADS // 2):
            c0 = pair * 2 * HEAD_DIM
            k2 = jnp.concatenate([k_ref[pl.ds(prev0, WINDOW), c0:c0 + 2 * HEAD_DIM],
                                  k_ref[pl.ds(cur0, WINDOW), c0:c0 + 2 * HEAD_DIM]], axis=0)
            v2 = jnp.concatenate([v_ref[pl.ds(prev0, WINDOW), c0:c0 + 2 * HEAD_DIM],
                                  v_ref[pl.ds(cur0, WINDOW), c0:c0 + 2 * HEAD_DIM]], axis=0)
            zkv = jnp.zeros_like(v2)
            v_aug = [jnp.concatenate([jnp.where(lane_kv < HEAD_DIM, v2, zkv), ones_kv], axis=1),
                     jnp.concatenate([jnp.where(lane_kv >= HEAD_DIM, v2, zkv), ones_kv], axis=1)]
            rows = []
            for g in range(SWA_GROUP):
                t0 = (pair * SWA_GROUP + g) * 2 * HEAD_DIM
                qt = q_ref[r0:r0 + WINDOW, t0:t0 + 2 * HEAD_DIM]
                zq = jnp.zeros_like(qt)
                rows.append(jnp.where(lane_q < HEAD_DIM, qt, zq))
                rows.append(jnp.where(lane_q >= HEAD_DIM, qt, zq))
            q8 = jnp.concatenate(rows, axis=0)
            s = lax.dot_general(q8, k2, (((1,), (1,)), ((), ())), preferred_element_type=F32)
            for g in range(SWA_GROUP):
                o = None
                for e in range(2):
                    rb = (g * 2 + e) * WINDOW
                    sb = s[rb:rb + WINDOW] + tbl_ref[pair, rb:rb + WINDOW, :]
                    if sub == 0:
                        sb = sb + jnp.where((kcol < WINDOW) & (blk == 0), NEG_INF, 0.0)
                    sink = jnp.full((WINDOW, lanes), sink_ref[(2 * pair + e) * SWA_GROUP + g] * LOG2E, F32)
                    m = jnp.maximum(jnp.max(sb, axis=1, keepdims=True), sink)
                    p = jnp.exp2(sb - jnp.concatenate([m, m], axis=1))
                    pv = jnp.dot(p.astype(BF16), v_aug[e], preferred_element_type=F32)
                    part = pv[:, :lanes] / (pv[:, lanes:] + jnp.exp2(sink - m))
                    o = part if o is None else o + part
                t0 = (pair * SWA_GROUP + g) * 2 * HEAD_DIM
                o_ref[r0:r0 + WINDOW, t0:t0 + 2 * HEAD_DIM] = o.astype(o_ref.dtype)


def _swa_attention(proj, sinks, tbl, *, batch, seq, tq):
    nq = seq // tq
    q_blk = (3 * DIFF_WIDTH) // SWA_WIDTH
    kv_w = SWA_KV_HEADS * HEAD_DIM
    k_blk = (3 * DIFF_WIDTH + SWA_WIDTH) // kv_w
    return pl.pallas_call(
        functools.partial(_swa_kernel, tq=tq),
        out_shape=jax.ShapeDtypeStruct((batch * seq, SWA_WIDTH), BF16),
        grid=(batch, nq),
        in_specs=[
            pl.BlockSpec(memory_space=pltpu.SMEM),
            pl.BlockSpec(tbl.shape, lambda b, i: (0, 0, 0)),
            pl.BlockSpec((tq, SWA_WIDTH), lambda b, i: (b * nq + i, q_blk)),
            pl.BlockSpec((seq, kv_w), lambda b, i: (b, k_blk)),
            pl.BlockSpec((seq, kv_w), lambda b, i: (b, k_blk + 1)),
        ],
        out_specs=pl.BlockSpec((tq, SWA_WIDTH), lambda b, i: (b * nq + i, 0)),
        compiler_params=pltpu.CompilerParams(
            dimension_semantics=("parallel", "arbitrary"),
            vmem_limit_bytes=VMEM_LIMIT_BYTES,
        ),
        name="swa_attention",
    )(sinks, tbl, proj, proj, proj)


def _outproj_kernel(x_ref, oa_ref, ob_ref, wa_ref, wb_ref, nw_ref, x1_ref, h_ref, *, rc):
    for c in range(x_ref.shape[0] // rc):
        rows = slice(c * rc, (c + 1) * rc)
        y = jnp.dot(oa_ref[rows, :], wa_ref[...], preferred_element_type=F32)
        y = y + jnp.dot(ob_ref[rows, :], wb_ref[...], preferred_element_type=F32)
        x1 = x_ref[rows, :] + y
        x1_ref[rows, :] = x1
        h_ref[rows, :] = (_rms_scale(x1) * nw_ref[...]).astype(h_ref.dtype)


def _outproj(x2, oa, ob, w, norm_w, *, tm, rc):
    tokens, d = x2.shape
    resident = dict(pipeline_mode=pl.Buffered(1))
    return pl.pallas_call(
        functools.partial(_outproj_kernel, rc=rc),
        out_shape=(jax.ShapeDtypeStruct((tokens, d), F32), jax.ShapeDtypeStruct((tokens, d), BF16)),
        grid=(tokens // tm,),
        in_specs=[
            pl.BlockSpec((tm, d), lambda i: (i, 0)),
            pl.BlockSpec((tm, DIFF_WIDTH), lambda i: (i, 0)),
            pl.BlockSpec((tm, SWA_WIDTH), lambda i: (i, 0)),
            pl.BlockSpec((DIFF_WIDTH, d), lambda i: (0, 0), **resident),
            pl.BlockSpec((SWA_WIDTH, d), lambda i: (DIFF_WIDTH // SWA_WIDTH, 0), **resident),
            pl.BlockSpec((1, d), lambda i: (0, 0), **resident),
        ],
        out_specs=(pl.BlockSpec((tm, d), lambda i: (i, 0)), pl.BlockSpec((tm, d), lambda i: (i, 0))),
        compiler_params=pltpu.CompilerParams(
            dimension_semantics=("parallel",),
            vmem_limit_bytes=VMEM_LIMIT_BYTES,
        ),
        name="outproj_residual_norm",
    )(x2, oa, ob, w, w, norm_w)


def _ffn_kernel(h_ref, x1_ref, wg_ref, wu_ref, wd_ref, nw_ref, o_ref, *, rc):
    f = pl.program_id(1)

    @pl.when(f == 0)
    def _():
        o_ref[...] = x1_ref[...]

    def gated(c):
        h = h_ref[c * rc:(c + 1) * rc, :]
        g = jnp.dot(h, wg_ref[...], preferred_element_type=F32)
        u = jnp.dot(h, wu_ref[...], preferred_element_type=F32)
        return (g * (1.0 / (1.0 + jnp.exp(-g))) * u).astype(BF16)

    n_chunk = h_ref.shape[0] // rc
    a_next = gated(0)
    for c in range(n_chunk):
        a = a_next
        if c + 1 < n_chunk:
            a_next = gated(c + 1)
        o_ref[c * rc:(c + 1) * rc, :] += jnp.dot(a, wd_ref[...], preferred_element_type=F32)

    @pl.when(f == pl.num_programs(1) - 1)
    def _():
        o_ref[...] = _rms_scale(o_ref[...]) * nw_ref[...]


def _ffn(h2, x1, wg, wu, wd, norm_w, *, tm, tf, rc):
    tokens, d = x1.shape
    dff = wd.shape[0]
    return pl.pallas_call(
        functools.partial(_ffn_kernel, rc=rc),
        out_shape=jax.ShapeDtypeStruct((tokens, d), F32),
        grid=(tokens // tm, dff // tf),
        in_specs=[
            pl.BlockSpec((tm, d), lambda i, f: (i, 0)),
            pl.BlockSpec((tm, d), lambda i, f: (i, 0)),
            pl.BlockSpec((None, d, tf), lambda i, f: (f, 0, 0)),
            pl.BlockSpec((None, d, tf), lambda i, f: (f, 0, 0)),
            pl.BlockSpec((tf, d), lambda i, f: (f, 0)),
            pl.BlockSpec((1, d), lambda i, f: (0, 0)),
        ],
        out_specs=pl.BlockSpec((tm, d), lambda i, f: (i, 0)),
        compiler_params=pltpu.CompilerParams(
            dimension_semantics=("parallel", "arbitrary"),
            vmem_limit_bytes=VMEM_LIMIT_BYTES,
        ),
        name="swiglu_ffn_final_norm",
    )(h2, x1, wg, wu, wd, norm_w)


def kernel(x, attn_norm_w, w_in, lambda_q1, lambda_k1, lambda_q2, lambda_k2, subln_w, sinks, w_out,
           ffn_norm_w, w_gate, w_up, w_down, final_norm_w):
    batch, seq, d = x.shape
    depth = w_in.shape[0]
    assert (d, w_in.shape[2], w_gate.shape[2]) == (D_MODEL, IN_COLS, D_FF)
    tokens = batch * seq
    x2 = x.reshape(tokens, d)

    q_cols = np.ones((1, IN_COLS), np.float32)
    q_cols[:, :DIFF_WIDTH] = Q_SCALE
    q_cols[:, 3 * DIFF_WIDTH:3 * DIFF_WIDTH + SWA_WIDTH] = Q_SCALE
    col_scale = jnp.asarray(q_cols)
    diff_slopes2 = jnp.asarray((_alibi_slopes(DIFF_HEADS) * LOG2E).astype(np.float32))
    swa_tbl = jnp.asarray(_swa_tables())

    assert depth == 1
    l = 0
    lambda_init = 0.8 - 0.6 * math.exp(-0.3 * l)
    proj, wg_b, wu_b, wd_b, wo_b = _norm_inproj(
        x2, attn_norm_w[l].reshape(1, d), col_scale, w_in[l].astype(BF16), w_gate[l], w_up[l], w_down[l],
        w_out[l], tm=512, tn=512, tf=512)
    oa = _diff_attention(
        proj, diff_slopes2,
        lambda_q1[l].reshape(1, HEAD_DIM), lambda_k1[l].reshape(1, HEAD_DIM),
        lambda_q2[l].reshape(1, HEAD_DIM), lambda_k2[l].reshape(1, HEAD_DIM),
        subln_w[l].reshape(1, 2 * HEAD_DIM),
        batch=batch, seq=seq, tk=512, rc=256, lambda_init=lambda_init)
    ob = _swa_attention(proj, sinks[l], swa_tbl, batch=batch, seq=seq, tq=512)
    x1, h2 = _outproj(x2, oa, ob, wo_b, ffn_norm_w[l].reshape(1, d), tm=512, rc=256)
    out = _ffn(h2, x1, wg_b, wu_b, wd_b, final_norm_w.reshape(1, d), tm=1024, tf=512, rc=256)
    return out.reshape(batch, seq, d)
```

```python
import functools
import math

import jax
import jax.numpy as jnp
import numpy as np
from jax import lax
from jax.experimental import pallas as pl
from jax.experimental.pallas import tpu as pltpu

D_MODEL = 2048
HEAD_DIM = 64
DIFF_HEADS = 8
DIFF_WIDTH = 1024
SWA_Q_HEADS = 16
SWA_KV_HEADS = 4
SWA_GROUP = 4
SWA_WIDTH = 1024
WINDOW = 128
D_FF = 5632
IN_COLS = 4608
RMS_EPS = 1e-5
LOG2E = math.log2(math.e)
Q_SCALE = HEAD_DIM ** -0.5 * LOG2E
NEG_INF = float("-inf")

V7X_VMEM_BYTES = 64 * 1024 * 1024
VMEM_LIMIT_BYTES = V7X_VMEM_BYTES - 8 * 1024 * 1024

ROW_TILE = 512
FFN_ROW_TILE = 1024
COL_TILE = 512
ROW_CHUNK = 256
KEY_BLOCK = 512

F32 = jnp.float32
BF16 = jnp.bfloat16


def _alibi_slopes(n_heads):
    return np.array([2.0 ** (-8.0 * (h + 1) / n_heads) for h in range(n_heads)], dtype=np.float64)


def _rms_scale(x):
    return x * lax.rsqrt(jnp.mean(x * x, axis=-1, keepdims=True) + RMS_EPS)


def _swa_query_reorder(acc):
    pieces = [acc[:, (e * SWA_GROUP + g) * HEAD_DIM:(e * SWA_GROUP + g + 1) * HEAD_DIM]
              for g in range(SWA_GROUP) for e in range(2)]
    return jnp.concatenate(pieces, axis=1)


def _norm_inproj_kernel(x_ref, nw_ref, cs_ref, w_ref, wg_ref, wu_ref, wd_ref, wo_ref,
                        o_ref, wg_o, wu_o, wd_o, wo_o, h_ref, *, tn):
    h_ref[...] = (_rms_scale(x_ref[...]) * nw_ref[...]).astype(BF16)
    for j in range(w_ref.shape[1] // tn):
        cols = slice(j * tn, (j + 1) * tn)
        acc = jnp.dot(h_ref[...], w_ref[:, cols], preferred_element_type=F32)
        if 3 * DIFF_WIDTH <= j * tn < 3 * DIFF_WIDTH + SWA_WIDTH:
            acc = _swa_query_reorder(acc)
        o_ref[:, cols] = (acc * cs_ref[:, cols]).astype(o_ref.dtype)
    tf = wg_o.shape[2]
    for f in range(wg_o.shape[0]):
        wg_o[f] = wg_ref[:, f * tf:(f + 1) * tf].astype(BF16)
        wu_o[f] = wu_ref[:, f * tf:(f + 1) * tf].astype(BF16)
    wd_o[...] = wd_ref[...].astype(BF16)
    wo_o[...] = wo_ref[...].astype(BF16)


def _swa_out_row_block(i):
    n_diff = DIFF_WIDTH // HEAD_DIM
    t = i - n_diff
    per_pair = 2 * SWA_GROUP
    src = n_diff + (t // per_pair) * per_pair + (t % 2) * SWA_GROUP + (t % per_pair) // 2
    return jnp.where(i < n_diff, i, src)


def _norm_inproj(x2, norm_w, col_scale, w_in_b, w_gate, w_up, w_down, w_out, *, tm, tn, tf):
    tokens, d = x2.shape
    n = w_in_b.shape[1]
    dff = w_gate.shape[1]
    steps = tokens // tm
    assert w_out.shape[0] == steps * HEAD_DIM and d % steps == 0 and dff % steps == 0
    assert tn == 2 * SWA_GROUP * HEAD_DIM
    resident = dict(pipeline_mode=pl.Buffered(1))
    rows = lambda r, c: pl.BlockSpec((r, c), lambda i: (i, 0))
    col_blocks = lambda: pl.BlockSpec((dff // tf, d // steps, tf), lambda i: (0, i, 0))
    return pl.pallas_call(
        functools.partial(_norm_inproj_kernel, tn=tn),
        out_shape=(
            jax.ShapeDtypeStruct((tokens, n), BF16),
            jax.ShapeDtypeStruct((dff // tf, d, tf), BF16),
            jax.ShapeDtypeStruct((dff // tf, d, tf), BF16),
            jax.ShapeDtypeStruct(w_down.shape, BF16),
            jax.ShapeDtypeStruct(w_out.shape, BF16),
        ),
        grid=(steps,),
        in_specs=[
            pl.BlockSpec((tm, d), lambda i: (i, 0)),
            pl.BlockSpec((1, d), lambda i: (0, 0), **resident),
            pl.BlockSpec((1, n), lambda i: (0, 0), **resident),
            pl.BlockSpec((d, n), lambda i: (0, 0), **resident),
            rows(d // steps, dff),
            rows(d // steps, dff),
            rows(dff // steps, d),
            pl.BlockSpec((HEAD_DIM, d), lambda i: (_swa_out_row_block(i), 0)),
        ],
        out_specs=(
            pl.BlockSpec((tm, n), lambda i: (i, 0)),
            col_blocks(),
            col_blocks(),
            rows(dff // steps, d),
            rows(HEAD_DIM, d),
        ),
        scratch_shapes=[pltpu.VMEM((tm, d), BF16)],
        compiler_params=pltpu.CompilerParams(
            dimension_semantics=("parallel",),
            vmem_limit_bytes=VMEM_LIMIT_BYTES,
        ),
        name="norm_inproj",
    )(x2, norm_w, col_scale, w_in_b, w_gate, w_up, w_down, w_out)


def _diff_attn_kernel(slope_ref, lq1_ref, lk1_ref, lq2_ref, lk2_ref, q_ref, k_ref, v_ref, sw_ref,
                      o_ref, m_ref, acc_ref, *, tk, rc, lambda_init):
    seq = q_ref.shape[0]
    lanes = 2 * HEAD_DIM
    n_blk = seq // tk
    nt = (((1,), (1,)), ((), ()))
    slope2 = slope_ref[pl.program_id(1)]

    first_map = lax.broadcasted_iota(jnp.int32, (rc, lanes), 1) < HEAD_DIM
    kcol = lax.broadcasted_iota(jnp.int32, (1, tk), 1).astype(F32)
    ones = jnp.ones((tk, lanes), BF16)
    tri = lax.broadcasted_iota(jnp.int32, (rc, rc), 1) <= lax.broadcasted_iota(jnp.int32, (rc, rc), 0)
    lam = (jnp.exp(jnp.sum(lq1_ref[...] * lk1_ref[...], axis=-1, keepdims=True))
           - jnp.exp(jnp.sum(lq2_ref[...] * lk2_ref[...], axis=-1, keepdims=True))
           + lambda_init)

    steps = [(i, j, sub, mp) for j in range(n_blk) for i in range(j, n_blk)
             for sub in range(tk // rc) for mp in range(2)]
    steps = [st for st in steps if st[0] > 0] + [st for st in steps if st[0] == 0]
    map1_out = {}

    def n_cols(i, j, sub):
        return sub * rc + rc if i == j else tk

    def logits(i, j, sub, mp):
        k0, r0, ncol = j * tk, i * tk + sub * rc, n_cols(i, j, sub)
        qc = q_ref[r0:r0 + rc, :]
        qz = jnp.where(first_map if mp == 0 else ~first_map, qc, jnp.zeros_like(qc))
        bias = slope2 * (kcol[:, :ncol] + float(k0 - i * tk))
        s = lax.dot_general(qz, k_ref[k0:k0 + ncol, :], nt, preferred_element_type=F32) + bias
        if i == j:
            s_diag = jnp.where(tri, s[:, ncol - rc:], NEG_INF)
            s = jnp.concatenate([s[:, :ncol - rc], s_diag], axis=1) if ncol > rc else s_diag
        return s

    def update(i, j, sub, mp, s):
        k0, r0, ncol = j * tk, i * tk + sub * rc, n_cols(i, j, sub)
        vaug = jnp.concatenate([v_ref[k0:k0 + ncol, :], ones[:ncol]], axis=1)
        row0 = mp * seq + r0
        rows = slice(row0, row0 + rc)
        m_cur = jnp.max(s, axis=1, keepdims=True)
        if j == 0:
            m_new = jnp.broadcast_to(m_cur, (rc, lanes))
        else:
            m_prev = m_ref[rows, :]
            m_new = jnp.maximum(m_prev, m_cur)
        p = jnp.exp2(s - jnp.concatenate([m_new] * (ncol // lanes), axis=1))
        pv = jnp.dot(p.astype(BF16), vaug, preferred_element_type=F32)
        if j > 0:
            alpha = jnp.exp2(m_prev - m_new)
            pv = jnp.concatenate([alpha, alpha], axis=1) * acc_ref[rows, :] + pv
        if i > j:
            m_ref[rows, :] = m_new
            acc_ref[rows, :] = pv
        elif mp == 0:
            map1_out[r0] = pv[:, :lanes] / pv[:, lanes:]
        else:
            o = map1_out.pop(r0) - lam * (pv[:, :lanes] / pv[:, lanes:])
            o = _rms_scale(o) * sw_ref[...] * (1.0 - lambda_init)
            o_ref[r0:r0 + rc, :] = o.astype(o_ref.dtype)

    ahead = 2
    pending = [logits(*st) for st in steps[:ahead]]
    for n, step in enumerate(steps):
        s_cur = pending.pop(0)
        if n + ahead < len(steps):
            pending.append(logits(*steps[n + ahead]))
        update(*step, s_cur)


def _diff_attention(proj, slopes2, lq1, lk1, lq2, lk2, subln_w, *, batch, seq, tk, rc, lambda_init):
    lane_blk = 2 * HEAD_DIM
    k_blk0 = DIFF_WIDTH // lane_blk
    v_blk0 = 2 * DIFF_WIDTH // lane_blk
    vec = lambda: pl.BlockSpec((1, HEAD_DIM), lambda b, h: (0, 0))
    return pl.pallas_call(
        functools.partial(_diff_attn_kernel, tk=tk, rc=rc, lambda_init=lambda_init),
        out_shape=jax.ShapeDtypeStruct((batch * seq, DIFF_WIDTH), BF16),
        grid=(batch, DIFF_HEADS),
        in_specs=[
            pl.BlockSpec(memory_space=pltpu.SMEM),
            vec(), vec(), vec(), vec(),
            pl.BlockSpec((seq, lane_blk), lambda b, h: (b, h)),
            pl.BlockSpec((seq, lane_blk), lambda b, h: (b, k_blk0 + h)),
            pl.BlockSpec((seq, lane_blk), lambda b, h: (b, v_blk0 + h)),
            pl.BlockSpec((1, lane_blk), lambda b, h: (0, 0)),
        ],
        out_specs=pl.BlockSpec((seq, lane_blk), lambda b, h: (b, h)),
        scratch_shapes=[
            pltpu.VMEM((2 * seq, lane_blk), F32),
            pltpu.VMEM((2 * seq, 2 * lane_blk), F32),
        ],
        compiler_params=pltpu.CompilerParams(
            dimension_semantics=("parallel", "parallel"),
            vmem_limit_bytes=VMEM_LIMIT_BYTES,
        ),
        name="diff_attention",
    )(slopes2, lq1, lk1, lq2, lk2, proj, proj, proj, subln_w)


def _swa_tables():
    slopes = _alibi_slopes(SWA_Q_HEADS)
    i = np.arange(WINDOW)[:, None]
    j = np.arange(2 * WINDOW)[None, :]
    dist = WINDOW + i - j
    valid = (dist >= 0) & (dist < WINDOW)
    tbl = np.empty((SWA_KV_HEADS // 2, SWA_GROUP * 2 * WINDOW, 2 * WINDOW), np.float32)
    for pair in range(SWA_KV_HEADS // 2):
        for g in range(SWA_GROUP):
            for e in range(2):
                head = (2 * pair + e) * SWA_GROUP + g
                r0 = (g * 2 + e) * WINDOW
                tbl[pair, r0:r0 + WINDOW] = np.where(valid, -slopes[head] * LOG2E * dist, -np.inf)
    return tbl


def _swa_kernel(sink_ref, tbl_ref, q_ref, k_ref, v_ref, o_ref, *, tq):
    qi = pl.program_id(1)
    n_sub = tq // WINDOW
    lanes = 2 * HEAD_DIM
    lane_q = lax.broadcasted_iota(jnp.int32, (WINDOW, lanes), 1)
    lane_kv = lax.broadcasted_iota(jnp.int32, (2 * WINDOW, lanes), 1)
    ones_kv = jnp.ones((2 * WINDOW, lanes), BF16)
    kcol = lax.broadcasted_iota(jnp.int32, (1, 2 * WINDOW), 1)
    for sub in range(n_sub):
        r0 = sub * WINDOW
        blk = qi * n_sub + sub
        cur0 = pl.multiple_of(blk * WINDOW, WINDOW)
        prev0 = pl.multiple_of(jnp.maximum(blk - 1, 0) * WINDOW, WINDOW)
        for pair in range(SWA_KV_HEADS // 2):
            c0 = pair * 2 * HEAD_DIM
            k2 = jnp.concatenate([k_ref[pl.ds(prev0, WINDOW), c0:c0 + 2 * HEAD_DIM],
                                  k_ref[pl.ds(cur0, WINDOW), c0:c0 + 2 * HEAD_DIM]], axis=0)
            v2 = jnp.concatenate([v_ref[pl.ds(prev0, WINDOW), c0:c0 + 2 * HEAD_DIM],
                                  v_ref[pl.ds(cur0, WINDOW), c0:c0 + 2 * HEAD_DIM]], axis=0)
            zkv = jnp.zeros_like(v2)
            v_aug = [jnp.concatenate([jnp.where(lane_kv < HEAD_DIM, v2, zkv), ones_kv], axis=1),
                     jnp.concatenate([jnp.where(lane_kv >= HEAD_DIM, v2, zkv), ones_kv], axis=1)]
            rows = []
            for g in range(SWA_GROUP):
                t0 = (pair * SWA_GROUP + g) * 2 * HEAD_DIM
                qt = q_ref[r0:r0 + WINDOW, t0:t0 + 2 * HEAD_DIM]
                zq = jnp.zeros_like(qt)
                rows.append(jnp.where(lane_q < HEAD_DIM, qt, zq))
                rows.append(jnp.where(lane_q >= HEAD_DIM, qt, zq))
            q8 = jnp.concatenate(rows, axis=0)
            s = lax.dot_general(q8, k2, (((1,), (1,)), ((), ())), preferred_element_type=F32)
            for g in range(SWA_GROUP):
                o = None
                for e in range(2):
                    rb = (g * 2 + e) * WINDOW
                    sb = s[rb:rb + WINDOW] + tbl_ref[pair, rb:rb + WINDOW, :]
                    if sub == 0:
                        sb = sb + jnp.where((kcol < WINDOW) & (blk == 0), NEG_INF, 0.0)
                    sink = jnp.full((WINDOW, lanes), sink_ref[(2 * pair + e) * SWA_GROUP + g] * LOG2E, F32)
                    m = jnp.maximum(jnp.max(sb, axis=1, keepdims=True), sink)
                    p = jnp.exp2(sb - jnp.concatenate([m, m], axis=1))
                    pv = jnp.dot(p.astype(BF16), v_aug[e], preferred_element_type=F32)
                    part = pv[:, :lanes] / (pv[:, lanes:] + jnp.exp2(sink - m))
                    o = part if o is None else o + part
                t0 = (pair * SWA_GROUP + g) * 2 * HEAD_DIM
                o_ref[r0:r0 + WINDOW, t0:t0 + 2 * HEAD_DIM] = o.astype(o_ref.dtype)


def _swa_attention(proj, sinks, tbl, *, batch, seq, tq):
    nq = seq // tq
    q_blk = (3 * DIFF_WIDTH) // SWA_WIDTH
    kv_w = SWA_KV_HEADS * HEAD_DIM
    k_blk = (3 * DIFF_WIDTH + SWA_WIDTH) // kv_w
    return pl.pallas_call(
        functools.partial(_swa_kernel, tq=tq),
        out_shape=jax.ShapeDtypeStruct((batch * seq, SWA_WIDTH), BF16),
        grid=(batch, nq),
        in_specs=[
            pl.BlockSpec(memory_space=pltpu.SMEM),
            pl.BlockSpec(tbl.shape, lambda b, i: (0, 0, 0)),
            pl.BlockSpec((tq, SWA_WIDTH), lambda b, i: (b * nq + i, q_blk)),
            pl.BlockSpec((seq, kv_w), lambda b, i: (b, k_blk)),
            pl.BlockSpec((seq, kv_w), lambda b, i: (b, k_blk + 1)),
        ],
        out_specs=pl.BlockSpec((tq, SWA_WIDTH), lambda b, i: (b * nq + i, 0)),
        compiler_params=pltpu.CompilerParams(
            dimension_semantics=("parallel", "arbitrary"),
            vmem_limit_bytes=VMEM_LIMIT_BYTES,
        ),
        name="swa_attention",
    )(sinks, tbl, proj, proj, proj)


def _outproj_kernel(x_ref, oa_ref, ob_ref, wa_ref, wb_ref, nw_ref, x1_ref, h_ref, *, rc):
    for c in range(x_ref.shape[0] // rc):
        rows = slice(c * rc, (c + 1) * rc)
        y = jnp.dot(oa_ref[rows, :], wa_ref[...], preferred_element_type=F32)
        y = y + jnp.dot(ob_ref[rows, :], wb_ref[...], preferred_element_type=F32)
        x1 = x_ref[rows, :] + y
        x1_ref[rows, :] = x1
        h_ref[rows, :] = (_rms_scale(x1) * nw_ref[...]).astype(h_ref.dtype)


def _outproj(x2, oa, ob, w, norm_w, *, tm, rc):
    tokens, d = x2.shape
    resident = dict(pipeline_mode=pl.Buffered(1))
    return pl.pallas_call(
        functools.partial(_outproj_kernel, rc=rc),
        out_shape=(jax.ShapeDtypeStruct((tokens, d), F32), jax.ShapeDtypeStruct((tokens, d), BF16)),
        grid=(tokens // tm,),
        in_specs=[
            pl.BlockSpec((tm, d), lambda i: (i, 0)),
            pl.BlockSpec((tm, DIFF_WIDTH), lambda i: (i, 0)),
            pl.BlockSpec((tm, SWA_WIDTH), lambda i: (i, 0)),
            pl.BlockSpec((DIFF_WIDTH, d), lambda i: (0, 0), **resident),
            pl.BlockSpec((SWA_WIDTH, d), lambda i: (DIFF_WIDTH // SWA_WIDTH, 0), **resident),
            pl.BlockSpec((1, d), lambda i: (0, 0), **resident),
        ],
        out_specs=(pl.BlockSpec((tm, d), lambda i: (i, 0)), pl.BlockSpec((tm, d), lambda i: (i, 0))),
        compiler_params=pltpu.CompilerParams(
            dimension_semantics=("parallel",),
            vmem_limit_bytes=VMEM_LIMIT_BYTES,
        ),
        name="outproj_residual_norm",
    )(x2, oa, ob, w, w, norm_w)


def _ffn_kernel(h_ref, x1_ref, wg_ref, wu_ref, wd_ref, nw_ref, o_ref, *, rc):
    f = pl.program_id(1)

    @pl.when(f == 0)
    def _():
        o_ref[...] = x1_ref[...]

    def gated(c):
        h = h_ref[c * rc:(c + 1) * rc, :]
        g = jnp.dot(h, wg_ref[...], preferred_element_type=F32)
        u = jnp.dot(h, wu_ref[...], preferred_element_type=F32)
        return (g * (1.0 / (1.0 + jnp.exp(-g))) * u).astype(BF16)

    n_chunk = h_ref.shape[0] // rc
    a_next = gated(0)
    for c in range(n_chunk):
        a = a_next
        if c + 1 < n_chunk:
            a_next = gated(c + 1)
        o_ref[c * rc:(c + 1) * rc, :] += jnp.dot(a, wd_ref[...], preferred_element_type=F32)

    @pl.when(f == pl.num_programs(1) - 1)
    def _():
        o_ref[...] = _rms_scale(o_ref[...]) * nw_ref[...]


def _ffn(h2, x1, wg, wu, wd, norm_w, *, tm, tf, rc):
    tokens, d = x1.shape
    dff = wd.shape[0]
    return pl.pallas_call(
        functools.partial(_ffn_kernel, rc=rc),
        out_shape=jax.ShapeDtypeStruct((tokens, d), F32),
        grid=(tokens // tm, dff // tf),
        in_specs=[
            pl.BlockSpec((tm, d), lambda i, f: (i, 0)),
            pl.BlockSpec((tm, d), lambda i, f: (i, 0)),
            pl.BlockSpec((None, d, tf), lambda i, f: (f, 0, 0)),
            pl.BlockSpec((None, d, tf), lambda i, f: (f, 0, 0)),
            pl.BlockSpec((tf, d), lambda i, f: (f, 0)),
            pl.BlockSpec((1, d), lambda i, f: (0, 0)),
        ],
        out_specs=pl.BlockSpec((tm, d), lambda i, f: (i, 0)),
        compiler_params=pltpu.CompilerParams(
            dimension_semantics=("parallel", "arbitrary"),
            vmem_limit_bytes=VMEM_LIMIT_BYTES,
        ),
        name="swiglu_ffn_final_norm",
    )(h2, x1, wg, wu, wd, norm_w)


def kernel(x, attn_norm_w, w_in, lambda_q1, lambda_k1, lambda_q2, lambda_k2, subln_w, sinks, w_out,
           ffn_norm_w, w_gate, w_up, w_down, final_norm_w):
    batch, seq, d = x.shape
    depth = w_in.shape[0]
    assert (d, w_in.shape[2], w_gate.shape[2]) == (D_MODEL, IN_COLS, D_FF)
    tokens = batch * seq
    assert seq % KEY_BLOCK == 0 and seq % ROW_TILE == 0 and tokens % FFN_ROW_TILE == 0
    x2 = x.reshape(tokens, d)

    q_cols = np.ones((1, IN_COLS), np.float32)
    q_cols[:, :DIFF_WIDTH] = Q_SCALE
    q_cols[:, 3 * DIFF_WIDTH:3 * DIFF_WIDTH + SWA_WIDTH] = Q_SCALE
    col_scale = jnp.asarray(q_cols)
    diff_slopes2 = jnp.asarray((_alibi_slopes(DIFF_HEADS) * LOG2E).astype(np.float32))
    swa_tbl = jnp.asarray(_swa_tables())

    assert depth == 1
    l = 0
    lambda_init = 0.8 - 0.6 * math.exp(-0.3 * l)
    proj, wg_b, wu_b, wd_b, wo_b = _norm_inproj(
        x2, attn_norm_w[l].reshape(1, d), col_scale, w_in[l].astype(BF16), w_gate[l], w_up[l], w_down[l],
        w_out[l], tm=ROW_TILE, tn=COL_TILE, tf=COL_TILE)
    oa = _diff_attention(
        proj, diff_slopes2,
        lambda_q1[l].reshape(1, HEAD_DIM), lambda_k1[l].reshape(1, HEAD_DIM),
        lambda_q2[l].reshape(1, HEAD_DIM), lambda_k2[l].reshape(1, HEAD_DIM),
        subln_w[l].reshape(1, 2 * HEAD_DIM),
        batch=batch, seq=seq, tk=KEY_BLOCK, rc=ROW_CHUNK, lambda_init=lambda_init)
    ob = _swa_attention(proj, sinks[l], swa_tbl, batch=batch, seq=seq, tq=ROW_TILE)
    x1, h2 = _outproj(x2, oa, ob, wo_b, ffn_norm_w[l].reshape(1, d), tm=ROW_TILE, rc=ROW_CHUNK)
    out = _ffn(h2, x1, wg_b, wu_b, wd_b, final_norm_w.reshape(1, d), tm=FFN_ROW_TILE, tf=COL_TILE, rc=ROW_CHUNK)
    return out.reshape(batch, seq, d)
```

```python
import functools
import math

import jax
import jax.numpy as jnp
import numpy as np
from jax import lax
from jax.experimental import pallas as pl
from jax.experimental.pallas import tpu as pltpu

D_MODEL = 2048
HEAD_DIM = 64
DIFF_HEADS = 8
DIFF_WIDTH = 1024
SWA_Q_HEADS = 16
SWA_KV_HEADS = 4
SWA_GROUP = 4
SWA_WIDTH = 1024
WINDOW = 128
D_FF = 5632
IN_COLS = 4608
RMS_EPS = 1e-5
LOG2E = math.log2(math.e)
Q_SCALE = HEAD_DIM ** -0.5 * LOG2E
NEG_INF = float("-inf")

V7X_VMEM_BYTES = 64 * 1024 * 1024
VMEM_LIMIT_BYTES = V7X_VMEM_BYTES - 8 * 1024 * 1024

ROW_TILE = 512
SWA_ROW_TILE = 1024
FFN_ROW_TILE = 1024
COL_TILE = 512
ROW_CHUNK = 256
KEY_BLOCK = 512

F32 = jnp.float32
BF16 = jnp.bfloat16


def _alibi_slopes(n_heads):
    return np.array([2.0 ** (-8.0 * (h + 1) / n_heads) for h in range(n_heads)], dtype=np.float64)


def _rms_scale(x):
    return x * lax.rsqrt(jnp.mean(x * x, axis=-1, keepdims=True) + RMS_EPS)


def _swa_query_reorder(acc):
    pieces = [acc[:, (e * SWA_GROUP + g) * HEAD_DIM:(e * SWA_GROUP + g + 1) * HEAD_DIM]
              for g in range(SWA_GROUP) for e in range(2)]
    return jnp.concatenate(pieces, axis=1)


def _norm_inproj_kernel(x_ref, nw_ref, cs_ref, w_ref, wg_ref, wu_ref, wd_ref, wo_ref,
                        o_ref, wg_o, wu_o, wd_o, wo_o, h_ref, *, tn):
    h_ref[...] = (_rms_scale(x_ref[...]) * nw_ref[...]).astype(BF16)
    for j in range(w_ref.shape[1] // tn):
        cols = slice(j * tn, (j + 1) * tn)
        acc = jnp.dot(h_ref[...], w_ref[:, cols], preferred_element_type=F32)
        if 3 * DIFF_WIDTH <= j * tn < 3 * DIFF_WIDTH + SWA_WIDTH:
            acc = _swa_query_reorder(acc)
        o_ref[:, cols] = (acc * cs_ref[:, cols]).astype(o_ref.dtype)
    tf = wg_o.shape[2]
    for f in range(wg_o.shape[0]):
        wg_o[f] = wg_ref[:, f * tf:(f + 1) * tf].astype(BF16)
        wu_o[f] = wu_ref[:, f * tf:(f + 1) * tf].astype(BF16)
    wd_o[...] = wd_ref[...].astype(BF16)
    wo_o[...] = wo_ref[...].astype(BF16)


def _swa_out_row_block(i):
    n_diff = DIFF_WIDTH // HEAD_DIM
    t = i - n_diff
    per_pair = 2 * SWA_GROUP
    src = n_diff + (t // per_pair) * per_pair + (t % 2) * SWA_GROUP + (t % per_pair) // 2
    return jnp.where(i < n_diff, i, src)


def _norm_inproj(x2, norm_w, col_scale, w_in_b, w_gate, w_up, w_down, w_out, *, tm, tn, tf):
    tokens, d = x2.shape
    n = w_in_b.shape[1]
    dff = w_gate.shape[1]
    steps = tokens // tm
    assert w_out.shape[0] == steps * HEAD_DIM and d % steps == 0 and dff % steps == 0
    assert tn == 2 * SWA_GROUP * HEAD_DIM
    resident = dict(pipeline_mode=pl.Buffered(1))
    rows = lambda r, c: pl.BlockSpec((r, c), lambda i: (i, 0))
    col_blocks = lambda: pl.BlockSpec((dff // tf, d // steps, tf), lambda i: (0, i, 0))
    return pl.pallas_call(
        functools.partial(_norm_inproj_kernel, tn=tn),
        out_shape=(
            jax.ShapeDtypeStruct((tokens, n), BF16),
            jax.ShapeDtypeStruct((dff // tf, d, tf), BF16),
            jax.ShapeDtypeStruct((dff // tf, d, tf), BF16),
            jax.ShapeDtypeStruct(w_down.shape, BF16),
            jax.ShapeDtypeStruct(w_out.shape, BF16),
        ),
        grid=(steps,),
        in_specs=[
            pl.BlockSpec((tm, d), lambda i: (i, 0)),
            pl.BlockSpec((1, d), lambda i: (0, 0), **resident),
            pl.BlockSpec((1, n), lambda i: (0, 0), **resident),
            pl.BlockSpec((d, n), lambda i: (0, 0), **resident),
            rows(d // steps, dff),
            rows(d // steps, dff),
            rows(dff // steps, d),
            pl.BlockSpec((HEAD_DIM, d), lambda i: (_swa_out_row_block(i), 0)),
        ],
        out_specs=(
            pl.BlockSpec((tm, n), lambda i: (i, 0)),
            col_blocks(),
            col_blocks(),
            rows(dff // steps, d),
            rows(HEAD_DIM, d),
        ),
        scratch_shapes=[pltpu.VMEM((tm, d), BF16)],
        compiler_params=pltpu.CompilerParams(
            dimension_semantics=("parallel",),
            vmem_limit_bytes=VMEM_LIMIT_BYTES,
        ),
        name="norm_inproj",
    )(x2, norm_w, col_scale, w_in_b, w_gate, w_up, w_down, w_out)


def _diff_attn_kernel(slope_ref, lq1_ref, lk1_ref, lq2_ref, lk2_ref, q_ref, k_ref, v_ref, sw_ref,
                      o_ref, m_ref, acc_ref, *, tk, rc, lambda_init):
    seq = q_ref.shape[0]
    lanes = 2 * HEAD_DIM
    n_blk = seq // tk
    nt = (((1,), (1,)), ((), ()))
    slope2 = slope_ref[pl.program_id(1)]

    first_map = lax.broadcasted_iota(jnp.int32, (rc, lanes), 1) < HEAD_DIM
    kcol = lax.broadcasted_iota(jnp.int32, (1, tk), 1).astype(F32)
    ones = jnp.ones((tk, lanes), BF16)
    tri = lax.broadcasted_iota(jnp.int32, (rc, rc), 1) <= lax.broadcasted_iota(jnp.int32, (rc, rc), 0)
    lam = (jnp.exp(jnp.sum(lq1_ref[...] * lk1_ref[...], axis=-1, keepdims=True))
           - jnp.exp(jnp.sum(lq2_ref[...] * lk2_ref[...], axis=-1, keepdims=True))
           + lambda_init)

    steps = [(i, j, sub, mp) for j in range(n_blk) for i in range(j, n_blk)
             for sub in range(tk // rc) for mp in range(2)]
    steps = [st for st in steps if st[0] > 0] + [st for st in steps if st[0] == 0]
    map1_out = {}

    def n_cols(i, j, sub):
        return sub * rc + rc if i == j else tk

    def logits(i, j, sub, mp):
        k0, r0, ncol = j * tk, i * tk + sub * rc, n_cols(i, j, sub)
        qc = q_ref[r0:r0 + rc, :]
        qz = jnp.where(first_map if mp == 0 else ~first_map, qc, jnp.zeros_like(qc))
        bias = slope2 * (kcol[:, :ncol] + float(k0 - i * tk))
        s = lax.dot_general(qz, k_ref[k0:k0 + ncol, :], nt, preferred_element_type=F32) + bias
        if i == j:
            s_diag = jnp.where(tri, s[:, ncol - rc:], NEG_INF)
            s = jnp.concatenate([s[:, :ncol - rc], s_diag], axis=1) if ncol > rc else s_diag
        return s

    def update(i, j, sub, mp, s):
        k0, r0, ncol = j * tk, i * tk + sub * rc, n_cols(i, j, sub)
        vaug = jnp.concatenate([v_ref[k0:k0 + ncol, :], ones[:ncol]], axis=1)
        row0 = mp * seq + r0
        rows = slice(row0, row0 + rc)
        m_cur = jnp.max(s, axis=1, keepdims=True)
        if j == 0:
            m_new = jnp.broadcast_to(m_cur, (rc, lanes))
        else:
            m_prev = m_ref[rows, :]
            m_new = jnp.maximum(m_prev, m_cur)
        p = jnp.exp2(s - jnp.concatenate([m_new] * (ncol // lanes), axis=1))
        pv = jnp.dot(p.astype(BF16), vaug, preferred_element_type=F32)
        if j > 0:
            alpha = jnp.exp2(m_prev - m_new)
            pv = jnp.concatenate([alpha, alpha], axis=1) * acc_ref[rows, :] + pv
        if i > j:
            m_ref[rows, :] = m_new
            acc_ref[rows, :] = pv
        elif mp == 0:
            map1_out[r0] = pv[:, :lanes] / pv[:, lanes:]
        else:
            o = map1_out.pop(r0) - lam * (pv[:, :lanes] / pv[:, lanes:])
            o = _rms_scale(o) * sw_ref[...] * (1.0 - lambda_init)
            o_ref[r0:r0 + rc, :] = o.astype(o_ref.dtype)

    ahead = 2
    pending = [logits(*st) for st in steps[:ahead]]
    for n, step in enumerate(steps):
        s_cur = pending.pop(0)
        if n + ahead < len(steps):
            pending.append(logits(*steps[n + ahead]))
        update(*step, s_cur)


def _diff_attention(proj, slopes2, lq1, lk1, lq2, lk2, subln_w, *, batch, seq, tk, rc, lambda_init):
    lane_blk = 2 * HEAD_DIM
    k_blk0 = DIFF_WIDTH // lane_blk
    v_blk0 = 2 * DIFF_WIDTH // lane_blk
    vec = lambda: pl.BlockSpec((1, HEAD_DIM), lambda b, h: (0, 0))
    return pl.pallas_call(
        functools.partial(_diff_attn_kernel, tk=tk, rc=rc, lambda_init=lambda_init),
        out_shape=jax.ShapeDtypeStruct((batch * seq, DIFF_WIDTH), BF16),
        grid=(batch, DIFF_HEADS),
        in_specs=[
            pl.BlockSpec(memory_space=pltpu.SMEM),
            vec(), vec(), vec(), vec(),
            pl.BlockSpec((seq, lane_blk), lambda b, h: (b, h)),
            pl.BlockSpec((seq, lane_blk), lambda b, h: (b, k_blk0 + h)),
            pl.BlockSpec((seq, lane_blk), lambda b, h: (b, v_blk0 + h)),
            pl.BlockSpec((1, lane_blk), lambda b, h: (0, 0)),
        ],
        out_specs=pl.BlockSpec((seq, lane_blk), lambda b, h: (b, h)),
        scratch_shapes=[
            pltpu.VMEM((2 * seq, lane_blk), F32),
            pltpu.VMEM((2 * seq, 2 * lane_blk), F32),
        ],
        compiler_params=pltpu.CompilerParams(
            dimension_semantics=("parallel", "parallel"),
            vmem_limit_bytes=VMEM_LIMIT_BYTES,
        ),
        name="diff_attention",
    )(slopes2, lq1, lk1, lq2, lk2, proj, proj, proj, subln_w)


def _swa_tables():
    slopes = _alibi_slopes(SWA_Q_HEADS)
    i = np.arange(WINDOW)[:, None]
    j = np.arange(2 * WINDOW)[None, :]
    dist = WINDOW + i - j
    valid = (dist >= 0) & (dist < WINDOW)
    tbl = np.empty((SWA_KV_HEADS // 2, SWA_GROUP * 2 * WINDOW, 2 * WINDOW), np.float32)
    for pair in range(SWA_KV_HEADS // 2):
        for g in range(SWA_GROUP):
            for e in range(2):
                head = (2 * pair + e) * SWA_GROUP + g
                r0 = (g * 2 + e) * WINDOW
                tbl[pair, r0:r0 + WINDOW] = np.where(valid, -slopes[head] * LOG2E * dist, -np.inf)
    return tbl


def _swa_kernel(sink_ref, tbl_ref, q_ref, k_ref, v_ref, o_ref, *, tq):
    qi = pl.program_id(1)
    n_sub = tq // WINDOW
    lanes = 2 * HEAD_DIM
    lane_q = lax.broadcasted_iota(jnp.int32, (WINDOW, lanes), 1)
    lane_kv = lax.broadcasted_iota(jnp.int32, (2 * WINDOW, lanes), 1)
    ones_kv = jnp.ones((2 * WINDOW, lanes), BF16)
    kcol = lax.broadcasted_iota(jnp.int32, (1, 2 * WINDOW), 1)
    for sub in range(n_sub):
        r0 = sub * WINDOW
        blk = qi * n_sub + sub
        cur0 = pl.multiple_of(blk * WINDOW, WINDOW)
        prev0 = pl.multiple_of(jnp.maximum(blk - 1, 0) * WINDOW, WINDOW)
        for pair in range(SWA_KV_HEADS // 2):
            c0 = pair * 2 * HEAD_DIM
            k2 = jnp.concatenate([k_ref[pl.ds(prev0, WINDOW), c0:c0 + 2 * HEAD_DIM],
                                  k_ref[pl.ds(cur0, WINDOW), c0:c0 + 2 * HEAD_DIM]], axis=0)
            v2 = jnp.concatenate([v_ref[pl.ds(prev0, WINDOW), c0:c0 + 2 * HEAD_DIM],
                                  v_ref[pl.ds(cur0, WINDOW), c0:c0 + 2 * HEAD_DIM]], axis=0)
            zkv = jnp.zeros_like(v2)
            v_aug = [jnp.concatenate([jnp.where(lane_kv < HEAD_DIM, v2, zkv), ones_kv], axis=1),
                     jnp.concatenate([jnp.where(lane_kv >= HEAD_DIM, v2, zkv), ones_kv], axis=1)]
            rows = []
            for g in range(SWA_GROUP):
                t0 = (pair * SWA_GROUP + g) * 2 * HEAD_DIM
                qt = q_ref[r0:r0 + WINDOW, t0:t0 + 2 * HEAD_DIM]
                zq = jnp.zeros_like(qt)
                rows.append(jnp.where(lane_q < HEAD_DIM, qt, zq))
                rows.append(jnp.where(lane_q >= HEAD_DIM, qt, zq))
            q8 = jnp.concatenate(rows, axis=0)
            s = lax.dot_general(q8, k2, (((1,), (1,)), ((), ())), preferred_element_type=F32)
            for g in range(SWA_GROUP):
                o = None
                for e in range(2):
                    rb = (g * 2 + e) * WINDOW
                    sb = s[rb:rb + WINDOW] + tbl_ref[pair, rb:rb + WINDOW, :]
                    if sub == 0:
                        sb = sb + jnp.where((kcol < WINDOW) & (blk == 0), NEG_INF, 0.0)
                    sink = jnp.full((WINDOW, lanes), sink_ref[(2 * pair + e) * SWA_GROUP + g] * LOG2E, F32)
                    m = jnp.maximum(jnp.max(sb, axis=1, keepdims=True), sink)
                    p = jnp.exp2(sb - jnp.concatenate([m, m], axis=1))
                    pv = jnp.dot(p.astype(BF16), v_aug[e], preferred_element_type=F32)
                    part = pv[:, :lanes] / (pv[:, lanes:] + jnp.exp2(sink - m))
                    o = part if o is None else o + part
                t0 = (pair * SWA_GROUP + g) * 2 * HEAD_DIM
                o_ref[r0:r0 + WINDOW, t0:t0 + 2 * HEAD_DIM] = o.astype(o_ref.dtype)


def _swa_attention(proj, sinks, tbl, *, batch, seq, tq):
    nq = seq // tq
    q_blk = (3 * DIFF_WIDTH) // SWA_WIDTH
    kv_w = SWA_KV_HEADS * HEAD_DIM
    k_blk = (3 * DIFF_WIDTH + SWA_WIDTH) // kv_w
    return pl.pallas_call(
        functools.partial(_swa_kernel, tq=tq),
        out_shape=jax.ShapeDtypeStruct((batch * seq, SWA_WIDTH), BF16),
        grid=(batch, nq),
        in_specs=[
            pl.BlockSpec(memory_space=pltpu.SMEM),
            pl.BlockSpec(tbl.shape, lambda b, i: (0, 0, 0)),
            pl.BlockSpec((tq, SWA_WIDTH), lambda b, i: (b * nq + i, q_blk)),
            pl.BlockSpec((seq, kv_w), lambda b, i: (b, k_blk)),
            pl.BlockSpec((seq, kv_w), lambda b, i: (b, k_blk + 1)),
        ],
        out_specs=pl.BlockSpec((tq, SWA_WIDTH), lambda b, i: (b * nq + i, 0)),
        compiler_params=pltpu.CompilerParams(
            dimension_semantics=("parallel", "arbitrary"),
            vmem_limit_bytes=VMEM_LIMIT_BYTES,
        ),
        name="swa_attention",
    )(sinks, tbl, proj, proj, proj)


def _outproj_kernel(x_ref, oa_ref, ob_ref, wa_ref, wb_ref, nw_ref, x1_ref, h_ref, *, rc):
    for c in range(x_ref.shape[0] // rc):
        rows = slice(c * rc, (c + 1) * rc)
        y = jnp.dot(oa_ref[rows, :], wa_ref[...], preferred_element_type=F32)
        y = y + jnp.dot(ob_ref[rows, :], wb_ref[...], preferred_element_type=F32)
        x1 = x_ref[rows, :] + y
        x1_ref[rows, :] = x1
        h_ref[rows, :] = (_rms_scale(x1) * nw_ref[...]).astype(h_ref.dtype)


def _outproj(x2, oa, ob, w, norm_w, *, tm, rc):
    tokens, d = x2.shape
    resident = dict(pipeline_mode=pl.Buffered(1))
    return pl.pallas_call(
        functools.partial(_outproj_kernel, rc=rc),
        out_shape=(jax.ShapeDtypeStruct((tokens, d), F32), jax.ShapeDtypeStruct((tokens, d), BF16)),
        grid=(tokens // tm,),
        in_specs=[
            pl.BlockSpec((tm, d), lambda i: (i, 0)),
            pl.BlockSpec((tm, DIFF_WIDTH), lambda i: (i, 0)),
            pl.BlockSpec((tm, SWA_WIDTH), lambda i: (i, 0)),
            pl.BlockSpec((DIFF_WIDTH, d), lambda i: (0, 0), **resident),
            pl.BlockSpec((SWA_WIDTH, d), lambda i: (DIFF_WIDTH // SWA_WIDTH, 0), **resident),
            pl.BlockSpec((1, d), lambda i: (0, 0), **resident),
        ],
        out_specs=(pl.BlockSpec((tm, d), lambda i: (i, 0)), pl.BlockSpec((tm, d), lambda i: (i, 0))),
        compiler_params=pltpu.CompilerParams(
            dimension_semantics=("parallel",),
            vmem_limit_bytes=VMEM_LIMIT_BYTES,
        ),
        name="outproj_residual_norm",
    )(x2, oa, ob, w, w, norm_w)


def _ffn_kernel(h_ref, x1_ref, wg_ref, wu_ref, wd_ref, nw_ref, o_ref, *, rc):
    f = pl.program_id(1)

    @pl.when(f == 0)
    def _():
        o_ref[...] = x1_ref[...]

    def gated(c):
        h = h_ref[c * rc:(c + 1) * rc, :]
        g = jnp.dot(h, wg_ref[...], preferred_element_type=F32)
        u = jnp.dot(h, wu_ref[...], preferred_element_type=F32)
        return (g * (1.0 / (1.0 + jnp.exp(-g))) * u).astype(BF16)

    n_chunk = h_ref.shape[0] // rc
    a_next = gated(0)
    for c in range(n_chunk):
        a = a_next
        if c + 1 < n_chunk:
            a_next = gated(c + 1)
        o_ref[c * rc:(c + 1) * rc, :] += jnp.dot(a, wd_ref[...], preferred_element_type=F32)

    @pl.when(f == pl.num_programs(1) - 1)
    def _():
        o_ref[...] = _rms_scale(o_ref[...]) * nw_ref[...]


def _ffn(h2, x1, wg, wu, wd, norm_w, *, tm, tf, rc):
    tokens, d = x1.shape
    dff = wd.shape[0]
    return pl.pallas_call(
        functools.partial(_ffn_kernel, rc=rc),
        out_shape=jax.ShapeDtypeStruct((tokens, d), F32),
        grid=(tokens // tm, dff // tf),
        in_specs=[
            pl.BlockSpec((tm, d), lambda i, f: (i, 0)),
            pl.BlockSpec((tm, d), lambda i, f: (i, 0)),
            pl.BlockSpec((None, d, tf), lambda i, f: (f, 0, 0)),
            pl.BlockSpec((None, d, tf), lambda i, f: (f, 0, 0)),
            pl.BlockSpec((tf, d), lambda i, f: (f, 0)),
            pl.BlockSpec((1, d), lambda i, f: (0, 0)),
        ],
        out_specs=pl.BlockSpec((tm, d), lambda i, f: (i, 0)),
        compiler_params=pltpu.CompilerParams(
            dimension_semantics=("parallel", "arbitrary"),
            vmem_limit_bytes=VMEM_LIMIT_BYTES,
        ),
        name="swiglu_ffn_final_norm",
    )(h2, x1, wg, wu, wd, norm_w)


def kernel(x, attn_norm_w, w_in, lambda_q1, lambda_k1, lambda_q2, lambda_k2, subln_w, sinks, w_out,
           ffn_norm_w, w_gate, w_up, w_down, final_norm_w):
    batch, seq, d = x.shape
    depth = w_in.shape[0]
    assert (d, w_in.shape[2], w_gate.shape[2]) == (D_MODEL, IN_COLS, D_FF)
    tokens = batch * seq
    assert seq % KEY_BLOCK == 0 and seq % SWA_ROW_TILE == 0
    assert tokens % ROW_TILE == 0 and tokens % FFN_ROW_TILE == 0
    x2 = x.reshape(tokens, d)

    q_cols = np.ones((1, IN_COLS), np.float32)
    q_cols[:, :DIFF_WIDTH] = Q_SCALE
    q_cols[:, 3 * DIFF_WIDTH:3 * DIFF_WIDTH + SWA_WIDTH] = Q_SCALE
    col_scale = jnp.asarray(q_cols)
    diff_slopes2 = jnp.asarray((_alibi_slopes(DIFF_HEADS) * LOG2E).astype(np.float32))
    swa_tbl = jnp.asarray(_swa_tables())

    assert depth == 1
    l = 0
    lambda_init = 0.8 - 0.6 * math.exp(-0.3 * l)
    proj, wg_b, wu_b, wd_b, wo_b = _norm_inproj(
        x2, attn_norm_w[l].reshape(1, d), col_scale, w_in[l].astype(BF16), w_gate[l], w_up[l], w_down[l],
        w_out[l], tm=ROW_TILE, tn=COL_TILE, tf=COL_TILE)
    oa = _diff_attention(
        proj, diff_slopes2,
        lambda_q1[l].reshape(1, HEAD_DIM), lambda_k1[l].reshape(1, HEAD_DIM),
        lambda_q2[l].reshape(1, HEAD_DIM), lambda_k2[l].reshape(1, HEAD_DIM),
        subln_w[l].reshape(1, 2 * HEAD_DIM),
        batch=batch, seq=seq, tk=KEY_BLOCK, rc=ROW_CHUNK, lambda_init=lambda_init)
    ob = _swa_attention(proj, sinks[l], swa_tbl, batch=batch, seq=seq, tq=SWA_ROW_TILE)
    x1, h2 = _outproj(x2, oa, ob, wo_b, ffn_norm_w[l].reshape(1, d), tm=ROW_TILE, rc=ROW_CHUNK)
    out = _ffn(h2, x1, wg_b, wu_b, wd_b, final_norm_w.reshape(1, d), tm=FFN_ROW_TILE, tf=COL_TILE, rc=ROW_CHUNK)
    return out.reshape(batch, seq, d)
```

```python
import functools
import math

import jax
import jax.numpy as jnp
import numpy as np
from jax import lax
from jax.experimental import pallas as pl
from jax.experimental.pallas import tpu as pltpu

D_MODEL = 2048
HEAD_DIM = 64
DIFF_HEADS = 8
DIFF_WIDTH = 1024
SWA_Q_HEADS = 16
SWA_KV_HEADS = 4
SWA_GROUP = 4
SWA_WIDTH = 1024
WINDOW = 128
D_FF = 5632
IN_COLS = 4608
RMS_EPS = 1e-5
LOG2E = math.log2(math.e)
Q_SCALE = HEAD_DIM ** -0.5 * LOG2E
NEG_INF = float("-inf")

V7X_VMEM_BYTES = 64 * 1024 * 1024
VMEM_LIMIT_BYTES = V7X_VMEM_BYTES - 8 * 1024 * 1024

ROW_TILE = 512
SWA_ROW_TILE = 2048
FFN_ROW_TILE = 1024
COL_TILE = 512
ROW_CHUNK = 256
KEY_BLOCK = 512

F32 = jnp.float32
BF16 = jnp.bfloat16


def _alibi_slopes(n_heads):
    return np.array([2.0 ** (-8.0 * (h + 1) / n_heads) for h in range(n_heads)], dtype=np.float64)


def _rms_scale(x):
    return x * lax.rsqrt(jnp.mean(x * x, axis=-1, keepdims=True) + RMS_EPS)


def _swa_query_reorder(acc):
    pieces = [acc[:, (e * SWA_GROUP + g) * HEAD_DIM:(e * SWA_GROUP + g + 1) * HEAD_DIM]
              for g in range(SWA_GROUP) for e in range(2)]
    return jnp.concatenate(pieces, axis=1)


def _norm_inproj_kernel(x_ref, nw_ref, cs_ref, w_ref, wg_ref, wu_ref, wd_ref, wo_ref,
                        o_ref, wg_o, wu_o, wd_o, wo_o, h_ref, *, tn):
    for c in range(x_ref.shape[0] // ROW_CHUNK):
        rows = slice(c * ROW_CHUNK, (c + 1) * ROW_CHUNK)
        h_ref[rows, :] = (_rms_scale(x_ref[rows, :]) * nw_ref[...]).astype(BF16)
        for j in range(w_ref.shape[1] // tn):
            cols = slice(j * tn, (j + 1) * tn)
            acc = jnp.dot(h_ref[rows, :], w_ref[:, cols], preferred_element_type=F32)
            if 3 * DIFF_WIDTH <= j * tn < 3 * DIFF_WIDTH + SWA_WIDTH:
                acc = _swa_query_reorder(acc)
            o_ref[rows, cols] = (acc * cs_ref[:, cols]).astype(o_ref.dtype)
    tf = wg_o.shape[2]
    for f in range(wg_o.shape[0]):
        wg_o[f] = wg_ref[:, f * tf:(f + 1) * tf].astype(BF16)
        wu_o[f] = wu_ref[:, f * tf:(f + 1) * tf].astype(BF16)
    wd_o[...] = wd_ref[...].astype(BF16)
    wo_o[...] = wo_ref[...].astype(BF16)


def _swa_out_row_block(i):
    n_diff = DIFF_WIDTH // HEAD_DIM
    t = i - n_diff
    per_pair = 2 * SWA_GROUP
    src = n_diff + (t // per_pair) * per_pair + (t % 2) * SWA_GROUP + (t % per_pair) // 2
    return jnp.where(i < n_diff, i, src)


def _norm_inproj(x2, norm_w, col_scale, w_in_b, w_gate, w_up, w_down, w_out, *, tm, tn, tf):
    tokens, d = x2.shape
    n = w_in_b.shape[1]
    dff = w_gate.shape[1]
    steps = tokens // tm
    assert w_out.shape[0] == steps * HEAD_DIM and d % steps == 0 and dff % steps == 0
    assert tn == 2 * SWA_GROUP * HEAD_DIM
    resident = dict(pipeline_mode=pl.Buffered(1))
    rows = lambda r, c: pl.BlockSpec((r, c), lambda i: (i, 0))
    col_blocks = lambda: pl.BlockSpec((dff // tf, d // steps, tf), lambda i: (0, i, 0))
    return pl.pallas_call(
        functools.partial(_norm_inproj_kernel, tn=tn),
        out_shape=(
            jax.ShapeDtypeStruct((tokens, n), BF16),
            jax.ShapeDtypeStruct((dff // tf, d, tf), BF16),
            jax.ShapeDtypeStruct((dff // tf, d, tf), BF16),
            jax.ShapeDtypeStruct(w_down.shape, BF16),
            jax.ShapeDtypeStruct(w_out.shape, BF16),
        ),
        grid=(steps,),
        in_specs=[
            pl.BlockSpec((tm, d), lambda i: (i, 0)),
            pl.BlockSpec((1, d), lambda i: (0, 0), **resident),
            pl.BlockSpec((1, n), lambda i: (0, 0), **resident),
            pl.BlockSpec((d, n), lambda i: (0, 0), **resident),
            rows(d // steps, dff),
            rows(d // steps, dff),
            rows(dff // steps, d),
            pl.BlockSpec((HEAD_DIM, d), lambda i: (_swa_out_row_block(i), 0)),
        ],
        out_specs=(
            pl.BlockSpec((tm, n), lambda i: (i, 0)),
            col_blocks(),
            col_blocks(),
            rows(dff // steps, d),
            rows(HEAD_DIM, d),
        ),
        scratch_shapes=[pltpu.VMEM((tm, d), BF16)],
        compiler_params=pltpu.CompilerParams(
            dimension_semantics=("parallel",),
            vmem_limit_bytes=VMEM_LIMIT_BYTES,
        ),
        name="norm_inproj",
    )(x2, norm_w, col_scale, w_in_b, w_gate, w_up, w_down, w_out)


def _diff_attn_kernel(slope_ref, lq1_ref, lk1_ref, lq2_ref, lk2_ref, q_ref, k_ref, v_ref, sw_ref,
                      o_ref, m_ref, acc_ref, *, tk, rc, lambda_init):
    seq = q_ref.shape[0]
    lanes = 2 * HEAD_DIM
    n_blk = seq // tk
    nt = (((1,), (1,)), ((), ()))
    slope2 = slope_ref[pl.program_id(1)]

    first_map = lax.broadcasted_iota(jnp.int32, (rc, lanes), 1) < HEAD_DIM
    kcol = lax.broadcasted_iota(jnp.int32, (1, tk), 1).astype(F32)
    ones = jnp.ones((tk, lanes), BF16)
    tri = lax.broadcasted_iota(jnp.int32, (rc, rc), 1) <= lax.broadcasted_iota(jnp.int32, (rc, rc), 0)
    lam = (jnp.exp(jnp.sum(lq1_ref[...] * lk1_ref[...], axis=-1, keepdims=True))
           - jnp.exp(jnp.sum(lq2_ref[...] * lk2_ref[...], axis=-1, keepdims=True))
           + lambda_init)

    steps = [(i, j, sub, mp) for j in range(n_blk) for i in range(j, n_blk)
             for sub in range(tk // rc) for mp in range(2)]
    steps = [st for st in steps if st[0] > 0] + [st for st in steps if st[0] == 0]
    map1_out = {}

    def n_cols(i, j, sub):
        return sub * rc + rc if i == j else tk

    def logits(i, j, sub, mp):
        k0, r0, ncol = j * tk, i * tk + sub * rc, n_cols(i, j, sub)
        qc = q_ref[r0:r0 + rc, :]
        qz = jnp.where(first_map if mp == 0 else ~first_map, qc, jnp.zeros_like(qc))
        bias = slope2 * (kcol[:, :ncol] + float(k0 - i * tk))
        s = lax.dot_general(qz, k_ref[k0:k0 + ncol, :], nt, preferred_element_type=F32) + bias
        if i == j:
            s_diag = jnp.where(tri, s[:, ncol - rc:], NEG_INF)
            s = jnp.concatenate([s[:, :ncol - rc], s_diag], axis=1) if ncol > rc else s_diag
        return s

    def update(i, j, sub, mp, s):
        k0, r0, ncol = j * tk, i * tk + sub * rc, n_cols(i, j, sub)
        vaug = jnp.concatenate([v_ref[k0:k0 + ncol, :], ones[:ncol]], axis=1)
        row0 = mp * seq + r0
        rows = slice(row0, row0 + rc)
        m_cur = jnp.max(s, axis=1, keepdims=True)
        if j == 0:
            m_new = jnp.broadcast_to(m_cur, (rc, lanes))
        else:
            m_prev = m_ref[rows, :]
            m_new = jnp.maximum(m_prev, m_cur)
        p = jnp.exp2(s - jnp.concatenate([m_new] * (ncol // lanes), axis=1))
        pv = jnp.dot(p.astype(BF16), vaug, preferred_element_type=F32)
        if j > 0:
            alpha = jnp.exp2(m_prev - m_new)
            pv = jnp.concatenate([alpha, alpha], axis=1) * acc_ref[rows, :] + pv
        if i > j:
            m_ref[rows, :] = m_new
            acc_ref[rows, :] = pv
        elif mp == 0:
            map1_out[r0] = pv[:, :lanes] / pv[:, lanes:]
        else:
            o = map1_out.pop(r0) - lam * (pv[:, :lanes] / pv[:, lanes:])
            o = _rms_scale(o) * sw_ref[...] * (1.0 - lambda_init)
            o_ref[r0:r0 + rc, :] = o.astype(o_ref.dtype)

    ahead = 2
    pending = [logits(*st) for st in steps[:ahead]]
    for n, step in enumerate(steps):
        s_cur = pending.pop(0)
        if n + ahead < len(steps):
            pending.append(logits(*steps[n + ahead]))
        update(*step, s_cur)


def _diff_attention(proj, slopes2, lq1, lk1, lq2, lk2, subln_w, *, batch, seq, tk, rc, lambda_init):
    lane_blk = 2 * HEAD_DIM
    k_blk0 = DIFF_WIDTH // lane_blk
    v_blk0 = 2 * DIFF_WIDTH // lane_blk
    vec = lambda: pl.BlockSpec((1, HEAD_DIM), lambda b, h: (0, 0))
    return pl.pallas_call(
        functools.partial(_diff_attn_kernel, tk=tk, rc=rc, lambda_init=lambda_init),
        out_shape=jax.ShapeDtypeStruct((batch * seq, DIFF_WIDTH), BF16),
        grid=(batch, DIFF_HEADS),
        in_specs=[
            pl.BlockSpec(memory_space=pltpu.SMEM),
            vec(), vec(), vec(), vec(),
            pl.BlockSpec((seq, lane_blk), lambda b, h: (b, h)),
            pl.BlockSpec((seq, lane_blk), lambda b, h: (b, k_blk0 + h)),
            pl.BlockSpec((seq, lane_blk), lambda b, h: (b, v_blk0 + h)),
            pl.BlockSpec((1, lane_blk), lambda b, h: (0, 0)),
        ],
        out_specs=pl.BlockSpec((seq, lane_blk), lambda b, h: (b, h)),
        scratch_shapes=[
            pltpu.VMEM((2 * seq, lane_blk), F32),
            pltpu.VMEM((2 * seq, 2 * lane_blk), F32),
        ],
        compiler_params=pltpu.CompilerParams(
            dimension_semantics=("parallel", "parallel"),
            vmem_limit_bytes=VMEM_LIMIT_BYTES,
        ),
        name="diff_attention",
    )(slopes2, lq1, lk1, lq2, lk2, proj, proj, proj, subln_w)


def _swa_tables():
    slopes = _alibi_slopes(SWA_Q_HEADS)
    i = np.arange(WINDOW)[:, None]
    j = np.arange(2 * WINDOW)[None, :]
    dist = WINDOW + i - j
    valid = (dist >= 0) & (dist < WINDOW)
    tbl = np.empty((SWA_KV_HEADS // 2, SWA_GROUP * 2 * WINDOW, 2 * WINDOW), np.float32)
    for pair in range(SWA_KV_HEADS // 2):
        for g in range(SWA_GROUP):
            for e in range(2):
                head = (2 * pair + e) * SWA_GROUP + g
                r0 = (g * 2 + e) * WINDOW
                tbl[pair, r0:r0 + WINDOW] = np.where(valid, -slopes[head] * LOG2E * dist, -np.inf)
    return tbl


def _swa_kernel(sink_ref, tbl_ref, q_ref, k_ref, v_ref, o_ref, *, tq):
    qi = pl.program_id(1)
    n_sub = tq // WINDOW
    lanes = 2 * HEAD_DIM
    lane_q = lax.broadcasted_iota(jnp.int32, (WINDOW, lanes), 1)
    lane_kv = lax.broadcasted_iota(jnp.int32, (2 * WINDOW, lanes), 1)
    ones_kv = jnp.ones((2 * WINDOW, lanes), BF16)
    kcol = lax.broadcasted_iota(jnp.int32, (1, 2 * WINDOW), 1)
    for sub in range(n_sub):
        r0 = sub * WINDOW
        blk = qi * n_sub + sub
        cur0 = pl.multiple_of(blk * WINDOW, WINDOW)
        prev0 = pl.multiple_of(jnp.maximum(blk - 1, 0) * WINDOW, WINDOW)
        for pair in range(SWA_KV_HEADS // 2):
            c0 = pair * 2 * HEAD_DIM
            k2 = jnp.concatenate([k_ref[pl.ds(prev0, WINDOW), c0:c0 + 2 * HEAD_DIM],
                                  k_ref[pl.ds(cur0, WINDOW), c0:c0 + 2 * HEAD_DIM]], axis=0)
            v2 = jnp.concatenate([v_ref[pl.ds(prev0, WINDOW), c0:c0 + 2 * HEAD_DIM],
                                  v_ref[pl.ds(cur0, WINDOW), c0:c0 + 2 * HEAD_DIM]], axis=0)
            zkv = jnp.zeros_like(v2)
            v_aug = [jnp.concatenate([jnp.where(lane_kv < HEAD_DIM, v2, zkv), ones_kv], axis=1),
                     jnp.concatenate([jnp.where(lane_kv >= HEAD_DIM, v2, zkv), ones_kv], axis=1)]
            rows = []
            for g in range(SWA_GROUP):
                t0 = (pair * SWA_GROUP + g) * 2 * HEAD_DIM
                qt = q_ref[r0:r0 + WINDOW, t0:t0 + 2 * HEAD_DIM]
                zq = jnp.zeros_like(qt)
                rows.append(jnp.where(lane_q < HEAD_DIM, qt, zq))
                rows.append(jnp.where(lane_q >= HEAD_DIM, qt, zq))
            q8 = jnp.concatenate(rows, axis=0)
            s = lax.dot_general(q8, k2, (((1,), (1,)), ((), ())), preferred_element_type=F32)
            for g in range(SWA_GROUP):
                o = None
                for e in range(2):
                    rb = (g * 2 + e) * WINDOW
                    sb = s[rb:rb + WINDOW] + tbl_ref[pair, rb:rb + WINDOW, :]
                    if sub == 0:
                        sb = sb + jnp.where((kcol < WINDOW) & (blk == 0), NEG_INF, 0.0)
                    sink = jnp.full((WINDOW, lanes), sink_ref[(2 * pair + e) * SWA_GROUP + g] * LOG2E, F32)
                    m = jnp.maximum(jnp.max(sb, axis=1, keepdims=True), sink)
                    p = jnp.exp2(sb - jnp.concatenate([m, m], axis=1))
                    pv = jnp.dot(p.astype(BF16), v_aug[e], preferred_element_type=F32)
                    part = pv[:, :lanes] / (pv[:, lanes:] + jnp.exp2(sink - m))
                    o = part if o is None else o + part
                t0 = (pair * SWA_GROUP + g) * 2 * HEAD_DIM
                o_ref[r0:r0 + WINDOW, t0:t0 + 2 * HEAD_DIM] = o.astype(o_ref.dtype)


def _swa_attention(proj, sinks, tbl, *, batch, seq, tq):
    nq = seq // tq
    q_blk = (3 * DIFF_WIDTH) // SWA_WIDTH
    kv_w = SWA_KV_HEADS * HEAD_DIM
    k_blk = (3 * DIFF_WIDTH + SWA_WIDTH) // kv_w
    return pl.pallas_call(
        functools.partial(_swa_kernel, tq=tq),
        out_shape=jax.ShapeDtypeStruct((batch * seq, SWA_WIDTH), BF16),
        grid=(batch, nq),
        in_specs=[
            pl.BlockSpec(memory_space=pltpu.SMEM),
            pl.BlockSpec(tbl.shape, lambda b, i: (0, 0, 0)),
            pl.BlockSpec((tq, SWA_WIDTH), lambda b, i: (b * nq + i, q_blk)),
            pl.BlockSpec((seq, kv_w), lambda b, i: (b, k_blk)),
            pl.BlockSpec((seq, kv_w), lambda b, i: (b, k_blk + 1)),
        ],
        out_specs=pl.BlockSpec((tq, SWA_WIDTH), lambda b, i: (b * nq + i, 0)),
        compiler_params=pltpu.CompilerParams(
            dimension_semantics=("parallel", "arbitrary"),
            vmem_limit_bytes=VMEM_LIMIT_BYTES,
        ),
        name="swa_attention",
    )(sinks, tbl, proj, proj, proj)


def _outproj_kernel(x_ref, oa_ref, ob_ref, wa_ref, wb_ref, nw_ref, x1_ref, h_ref, *, rc):
    for c in range(x_ref.shape[0] // rc):
        rows = slice(c * rc, (c + 1) * rc)
        y = jnp.dot(oa_ref[rows, :], wa_ref[...], preferred_element_type=F32)
        y = y + jnp.dot(ob_ref[rows, :], wb_ref[...], preferred_element_type=F32)
        x1 = x_ref[rows, :] + y
        x1_ref[rows, :] = x1
        h_ref[rows, :] = (_rms_scale(x1) * nw_ref[...]).astype(h_ref.dtype)


def _outproj(x2, oa, ob, w, norm_w, *, tm, rc):
    tokens, d = x2.shape
    resident = dict(pipeline_mode=pl.Buffered(1))
    return pl.pallas_call(
        functools.partial(_outproj_kernel, rc=rc),
        out_shape=(jax.ShapeDtypeStruct((tokens, d), F32), jax.ShapeDtypeStruct((tokens, d), BF16)),
        grid=(tokens // tm,),
        in_specs=[
            pl.BlockSpec((tm, d), lambda i: (i, 0)),
            pl.BlockSpec((tm, DIFF_WIDTH), lambda i: (i, 0)),
            pl.BlockSpec((tm, SWA_WIDTH), lambda i: (i, 0)),
            pl.BlockSpec((DIFF_WIDTH, d), lambda i: (0, 0), **resident),
            pl.BlockSpec((SWA_WIDTH, d), lambda i: (DIFF_WIDTH // SWA_WIDTH, 0), **resident),
            pl.BlockSpec((1, d), lambda i: (0, 0), **resident),
        ],
        out_specs=(pl.BlockSpec((tm, d), lambda i: (i, 0)), pl.BlockSpec((tm, d), lambda i: (i, 0))),
        compiler_params=pltpu.CompilerParams(
            dimension_semantics=("parallel",),
            vmem_limit_bytes=VMEM_LIMIT_BYTES,
        ),
        name="outproj_residual_norm",
    )(x2, oa, ob, w, w, norm_w)


def _ffn_kernel(h_ref, x1_ref, wg_ref, wu_ref, wd_ref, nw_ref, o_ref, *, rc):
    f = pl.program_id(1)

    @pl.when(f == 0)
    def _():
        o_ref[...] = x1_ref[...]

    def gated(c):
        h = h_ref[c * rc:(c + 1) * rc, :]
        g = jnp.dot(h, wg_ref[...], preferred_element_type=F32)
        u = jnp.dot(h, wu_ref[...], preferred_element_type=F32)
        return (g * (1.0 / (1.0 + jnp.exp(-g))) * u).astype(BF16)

    n_chunk = h_ref.shape[0] // rc
    a_next = gated(0)
    for c in range(n_chunk):
        a = a_next
        if c + 1 < n_chunk:
            a_next = gated(c + 1)
        o_ref[c * rc:(c + 1) * rc, :] += jnp.dot(a, wd_ref[...], preferred_element_type=F32)

    @pl.when(f == pl.num_programs(1) - 1)
    def _():
        o_ref[...] = _rms_scale(o_ref[...]) * nw_ref[...]


def _ffn(h2, x1, wg, wu, wd, norm_w, *, tm, tf, rc):
    tokens, d = x1.shape
    dff = wd.shape[0]
    return pl.pallas_call(
        functools.partial(_ffn_kernel, rc=rc),
        out_shape=jax.ShapeDtypeStruct((tokens, d), F32),
        grid=(tokens // tm, dff // tf),
        in_specs=[
            pl.BlockSpec((tm, d), lambda i, f: (i, 0)),
            pl.BlockSpec((tm, d), lambda i, f: (i, 0)),
            pl.BlockSpec((None, d, tf), lambda i, f: (f, 0, 0)),
            pl.BlockSpec((None, d, tf), lambda i, f: (f, 0, 0)),
            pl.BlockSpec((tf, d), lambda i, f: (f, 0)),
            pl.BlockSpec((1, d), lambda i, f: (0, 0)),
        ],
        out_specs=pl.BlockSpec((tm, d), lambda i, f: (i, 0)),
        compiler_params=pltpu.CompilerParams(
            dimension_semantics=("parallel", "arbitrary"),
            vmem_limit_bytes=VMEM_LIMIT_BYTES,
        ),
        name="swiglu_ffn_final_norm",
    )(h2, x1, wg, wu, wd, norm_w)


def kernel(x, attn_norm_w, w_in, lambda_q1, lambda_k1, lambda_q2, lambda_k2, subln_w, sinks, w_out,
           ffn_norm_w, w_gate, w_up, w_down, final_norm_w):
    batch, seq, d = x.shape
    depth = w_in.shape[0]
    assert (d, w_in.shape[2], w_gate.shape[2]) == (D_MODEL, IN_COLS, D_FF)
    tokens = batch * seq
    assert seq % KEY_BLOCK == 0 and seq % SWA_ROW_TILE == 0
    assert tokens % ROW_TILE == 0 and tokens % FFN_ROW_TILE == 0
    x2 = x.reshape(tokens, d)

    q_cols = np.ones((1, IN_COLS), np.float32)
    q_cols[:, :DIFF_WIDTH] = Q_SCALE
    q_cols[:, 3 * DIFF_WIDTH:3 * DIFF_WIDTH + SWA_WIDTH] = Q_SCALE
    col_scale = jnp.asarray(q_cols)
    diff_slopes2 = jnp.asarray((_alibi_slopes(DIFF_HEADS) * LOG2E).astype(np.float32))
    swa_tbl = jnp.asarray(_swa_tables())

    assert depth == 1
    l = 0
    lambda_init = 0.8 - 0.6 * math.exp(-0.3 * l)
    proj, wg_b, wu_b, wd_b, wo_b = _norm_inproj(
        x2, attn_norm_w[l].reshape(1, d), col_scale, w_in[l].astype(BF16), w_gate[l], w_up[l], w_down[l],
        w_out[l], tm=ROW_TILE, tn=COL_TILE, tf=COL_TILE)
    oa = _diff_attention(
        proj, diff_slopes2,
        lambda_q1[l].reshape(1, HEAD_DIM), lambda_k1[l].reshape(1, HEAD_DIM),
        lambda_q2[l].reshape(1, HEAD_DIM), lambda_k2[l].reshape(1, HEAD_DIM),
        subln_w[l].reshape(1, 2 * HEAD_DIM),
        batch=batch, seq=seq, tk=KEY_BLOCK, rc=ROW_CHUNK, lambda_init=lambda_init)
    ob = _swa_attention(proj, sinks[l], swa_tbl, batch=batch, seq=seq, tq=SWA_ROW_TILE)
    x1, h2 = _outproj(x2, oa, ob, wo_b, ffn_norm_w[l].reshape(1, d), tm=ROW_TILE, rc=ROW_CHUNK)
    out = _ffn(h2, x1, wg_b, wu_b, wd_b, final_norm_w.reshape(1, d), tm=FFN_ROW_TILE, tf=COL_TILE, rc=ROW_CHUNK)
    return out.reshape(batch, seq, d)
```

```python
import functools
import math

import jax
import jax.numpy as jnp
import numpy as np
from jax import lax
from jax.experimental import pallas as pl
from jax.experimental.pallas import tpu as pltpu

D_MODEL = 2048
HEAD_DIM = 64
DIFF_HEADS = 8
DIFF_WIDTH = 1024
SWA_Q_HEADS = 16
SWA_KV_HEADS = 4
SWA_GROUP = 4
SWA_WIDTH = 1024
WINDOW = 128
D_FF = 5632
IN_COLS = 4608
RMS_EPS = 1e-5
LOG2E = math.log2(math.e)
Q_SCALE = HEAD_DIM ** -0.5 * LOG2E
NEG_INF = float("-inf")

V7X_VMEM_BYTES = 64 * 1024 * 1024
VMEM_LIMIT_BYTES = V7X_VMEM_BYTES - 8 * 1024 * 1024

ROW_TILE = 512
SWA_ROW_TILE = 1024
FFN_ROW_TILE = 1024
COL_TILE = 512
ROW_CHUNK = 256
KEY_BLOCK = 512

F32 = jnp.float32
BF16 = jnp.bfloat16


def _alibi_slopes(n_heads):
    return np.array([2.0 ** (-8.0 * (h + 1) / n_heads) for h in range(n_heads)], dtype=np.float64)


def _rms_scale(x):
    return x * lax.rsqrt(jnp.mean(x * x, axis=-1, keepdims=True) + RMS_EPS)


def _swa_query_reorder(acc):
    pieces = [acc[:, (e * SWA_GROUP + g) * HEAD_DIM:(e * SWA_GROUP + g + 1) * HEAD_DIM]
              for g in range(SWA_GROUP) for e in range(2)]
    return jnp.concatenate(pieces, axis=1)


def _norm_inproj_kernel(x_ref, nw_ref, cs_ref, w_ref, wg_ref, wu_ref, wd_ref, wo_ref,
                        o_ref, wg_o, wu_o, wd_o, wo_o, h_ref, *, tn):
    h_ref[...] = (_rms_scale(x_ref[...]) * nw_ref[...]).astype(BF16)
    for j in range(w_ref.shape[1] // tn):
        cols = slice(j * tn, (j + 1) * tn)
        acc = jnp.dot(h_ref[...], w_ref[:, cols], preferred_element_type=F32)
        if 3 * DIFF_WIDTH <= j * tn < 3 * DIFF_WIDTH + SWA_WIDTH:
            acc = _swa_query_reorder(acc)
        o_ref[:, cols] = (acc * cs_ref[:, cols]).astype(o_ref.dtype)
    tf = wg_o.shape[2]
    for f in range(wg_o.shape[0]):
        wg_o[f] = wg_ref[:, f * tf:(f + 1) * tf].astype(BF16)
        wu_o[f] = wu_ref[:, f * tf:(f + 1) * tf].astype(BF16)
    wd_o[...] = wd_ref[...].astype(BF16)
    wo_o[...] = wo_ref[...].astype(BF16)


def _swa_out_row_block(i):
    n_diff = DIFF_WIDTH // HEAD_DIM
    t = i - n_diff
    per_pair = 2 * SWA_GROUP
    src = n_diff + (t // per_pair) * per_pair + (t % 2) * SWA_GROUP + (t % per_pair) // 2
    return jnp.where(i < n_diff, i, src)


def _norm_inproj(x2, norm_w, col_scale, w_in_b, w_gate, w_up, w_down, w_out, *, tm, tn, tf):
    tokens, d = x2.shape
    n = w_in_b.shape[1]
    dff = w_gate.shape[1]
    steps = tokens // tm
    assert w_out.shape[0] == steps * HEAD_DIM and d % steps == 0 and dff % steps == 0
    assert tn == 2 * SWA_GROUP * HEAD_DIM
    resident = dict(pipeline_mode=pl.Buffered(1))
    rows = lambda r, c: pl.BlockSpec((r, c), lambda i: (i, 0))
    col_blocks = lambda: pl.BlockSpec((dff // tf, d // steps, tf), lambda i: (0, i, 0))
    return pl.pallas_call(
        functools.partial(_norm_inproj_kernel, tn=tn),
        out_shape=(
            jax.ShapeDtypeStruct((tokens, n), BF16),
            jax.ShapeDtypeStruct((dff // tf, d, tf), BF16),
            jax.ShapeDtypeStruct((dff // tf, d, tf), BF16),
            jax.ShapeDtypeStruct(w_down.shape, BF16),
            jax.ShapeDtypeStruct(w_out.shape, BF16),
        ),
        grid=(steps,),
        in_specs=[
            pl.BlockSpec((tm, d), lambda i: (i, 0)),
            pl.BlockSpec((1, d), lambda i: (0, 0), **resident),
            pl.BlockSpec((1, n), lambda i: (0, 0), **resident),
            pl.BlockSpec((d, n), lambda i: (0, 0), **resident),
            rows(d // steps, dff),
            rows(d // steps, dff),
            rows(dff // steps, d),
            pl.BlockSpec((HEAD_DIM, d), lambda i: (_swa_out_row_block(i), 0)),
        ],
        out_specs=(
            pl.BlockSpec((tm, n), lambda i: (i, 0)),
            col_blocks(),
            col_blocks(),
            rows(dff // steps, d),
            rows(HEAD_DIM, d),
        ),
        scratch_shapes=[pltpu.VMEM((tm, d), BF16)],
        compiler_params=pltpu.CompilerParams(
            dimension_semantics=("parallel",),
            vmem_limit_bytes=VMEM_LIMIT_BYTES,
        ),
        name="norm_inproj",
    )(x2, norm_w, col_scale, w_in_b, w_gate, w_up, w_down, w_out)


def _split3_bf16(x):
    a = x.astype(BF16).astype(F32)
    b = (x - a).astype(BF16).astype(F32)
    c = (x - a - b).astype(BF16).astype(F32)
    return a, b, c


def _diff_attn_kernel(slope_ref, lq1_ref, lk1_ref, lq2_ref, lk2_ref, q_ref, k_ref, v_ref, sw_ref,
                      o_ref, m_ref, acc_ref, *, tk, rc, lambda_init):
    seq = q_ref.shape[0]
    lanes = 2 * HEAD_DIM
    n_blk = seq // tk
    nt = (((1,), (1,)), ((), ()))
    slope2 = slope_ref[pl.program_id(1)]

    first_map = lax.broadcasted_iota(jnp.int32, (rc, lanes), 1) < HEAD_DIM
    ones = jnp.ones((tk, lanes), BF16)
    tri = lax.broadcasted_iota(jnp.int32, (rc, rc), 1) <= lax.broadcasted_iota(jnp.int32, (rc, rc), 0)
    lam = (jnp.exp(jnp.sum(lq1_ref[...] * lk1_ref[...], axis=-1, keepdims=True))
           - jnp.exp(jnp.sum(lq2_ref[...] * lk2_ref[...], axis=-1, keepdims=True))
           + lambda_init)

    kp = lax.broadcasted_iota(jnp.int32, (tk, lanes), 0)
    fl = lax.broadcasted_iota(jnp.int32, (tk, lanes), 1)
    pos_feat = jnp.where(fl < 3, lax.shift_right_logical(kp, 4),
                         jnp.where(fl < 6, kp & 15, jnp.where(fl < 9, 1, 0))).astype(F32).astype(BF16)
    lane1 = lax.broadcasted_iota(jnp.int32, (1, lanes), 1)
    slope_vec = jnp.full((1, lanes), slope2, F32)
    s_parts = _split3_bf16(slope_vec)
    coef_cache = {}

    def coef(off):
        if off not in coef_cache:
            o_parts = _split3_bf16(slope_vec * float(off))
            row = jnp.zeros((1, lanes), F32)
            for t in range(3):
                row = jnp.where(lane1 == t, 16.0 * s_parts[t], row)
                row = jnp.where(lane1 == 3 + t, s_parts[t], row)
                row = jnp.where(lane1 == 6 + t, o_parts[t], row)
            coef_cache[off] = jnp.broadcast_to(row.astype(BF16), (rc, lanes))
        return coef_cache[off]

    steps = [(i, j, sub, mp) for j in range(n_blk) for i in range(j, n_blk)
             for sub in range(tk // rc) for mp in range(2)]
    steps = [st for st in steps if st[0] > 0] + [st for st in steps if st[0] == 0]
    map1_out = {}

    def n_cols(i, j, sub):
        return sub * rc + rc if i == j else tk

    def logits(i, j, sub, mp):
        k0, r0, ncol = j * tk, i * tk + sub * rc, n_cols(i, j, sub)
        qc = q_ref[r0:r0 + rc, :]
        qz = jnp.where(first_map if mp == 0 else ~first_map, qc, jnp.zeros_like(qc))
        q_aug = jnp.concatenate([qz, coef(k0 - i * tk)], axis=1)
        k_aug = jnp.concatenate([k_ref[k0:k0 + ncol, :], pos_feat[:ncol]], axis=1)
        s = lax.dot_general(q_aug, k_aug, nt, preferred_element_type=F32)
        if i == j:
            s_diag = jnp.where(tri, s[:, ncol - rc:], NEG_INF)
            s = jnp.concatenate([s[:, :ncol - rc], s_diag], axis=1) if ncol > rc else s_diag
        return s

    def update(i, j, sub, mp, s):
        k0, r0, ncol = j * tk, i * tk + sub * rc, n_cols(i, j, sub)
        vaug = jnp.concatenate([v_ref[k0:k0 + ncol, :], ones[:ncol]], axis=1)
        row0 = mp * seq + r0
        rows = slice(row0, row0 + rc)
        m_cur = jnp.max(s, axis=1, keepdims=True)
        if j == 0:
            m_new = jnp.broadcast_to(m_cur, (rc, lanes))
        else:
            m_prev = m_ref[rows, :]
            m_new = jnp.maximum(m_prev, m_cur)
        p = jnp.exp2(s - jnp.concatenate([m_new] * (ncol // lanes), axis=1))
        pv = jnp.dot(p.astype(BF16), vaug, preferred_element_type=F32)
        if j > 0:
            alpha = jnp.exp2(m_prev - m_new)
            pv = jnp.concatenate([alpha, alpha], axis=1) * acc_ref[rows, :] + pv
        if i > j:
            m_ref[rows, :] = m_new
            acc_ref[rows, :] = pv
        elif mp == 0:
            map1_out[r0] = pv[:, :lanes] / pv[:, lanes:]
        else:
            o = map1_out.pop(r0) - lam * (pv[:, :lanes] / pv[:, lanes:])
            o = _rms_scale(o) * sw_ref[...] * (1.0 - lambda_init)
            o_ref[r0:r0 + rc, :] = o.astype(o_ref.dtype)

    ahead = 2
    pending = [logits(*st) for st in steps[:ahead]]
    for n, step in enumerate(steps):
        s_cur = pending.pop(0)
        if n + ahead < len(steps):
            pending.append(logits(*steps[n + ahead]))
        update(*step, s_cur)


def _diff_attention(proj, slopes2, lq1, lk1, lq2, lk2, subln_w, *, batch, seq, tk, rc, lambda_init):
    lane_blk = 2 * HEAD_DIM
    k_blk0 = DIFF_WIDTH // lane_blk
    v_blk0 = 2 * DIFF_WIDTH // lane_blk
    vec = lambda: pl.BlockSpec((1, HEAD_DIM), lambda b, h: (0, 0))
    return pl.pallas_call(
        functools.partial(_diff_attn_kernel, tk=tk, rc=rc, lambda_init=lambda_init),
        out_shape=jax.ShapeDtypeStruct((batch * seq, DIFF_WIDTH), BF16),
        grid=(batch, DIFF_HEADS),
        in_specs=[
            pl.BlockSpec(memory_space=pltpu.SMEM),
            vec(), vec(), vec(), vec(),
            pl.BlockSpec((seq, lane_blk), lambda b, h: (b, h)),
            pl.BlockSpec((seq, lane_blk), lambda b, h: (b, k_blk0 + h)),
            pl.BlockSpec((seq, lane_blk), lambda b, h: (b, v_blk0 + h)),
            pl.BlockSpec((1, lane_blk), lambda b, h: (0, 0)),
        ],
        out_specs=pl.BlockSpec((seq, lane_blk), lambda b, h: (b, h)),
        scratch_shapes=[
            pltpu.VMEM((2 * seq, lane_blk), F32),
            pltpu.VMEM((2 * seq, 2 * lane_blk), F32),
        ],
        compiler_params=pltpu.CompilerParams(
            dimension_semantics=("parallel", "parallel"),
            vmem_limit_bytes=VMEM_LIMIT_BYTES,
        ),
        name="diff_attention",
    )(slopes2, lq1, lk1, lq2, lk2, proj, proj, proj, subln_w)


def _swa_tables():
    slopes = _alibi_slopes(SWA_Q_HEADS)
    i = np.arange(WINDOW)[:, None]
    j = np.arange(2 * WINDOW)[None, :]
    dist = WINDOW + i - j
    valid = (dist >= 0) & (dist < WINDOW)
    tbl = np.empty((SWA_KV_HEADS // 2, SWA_GROUP * 2 * WINDOW, 2 * WINDOW), np.float32)
    for pair in range(SWA_KV_HEADS // 2):
        for g in range(SWA_GROUP):
            for e in range(2):
                head = (2 * pair + e) * SWA_GROUP + g
                r0 = (g * 2 + e) * WINDOW
                tbl[pair, r0:r0 + WINDOW] = np.where(valid, -slopes[head] * LOG2E * dist, -np.inf)
    return tbl


def _swa_kernel(sink_ref, tbl_ref, q_ref, k_ref, v_ref, o_ref, *, tq):
    qi = pl.program_id(1)
    n_sub = tq // WINDOW
    lanes = 2 * HEAD_DIM
    lane_q = lax.broadcasted_iota(jnp.int32, (WINDOW, lanes), 1)
    lane_kv = lax.broadcasted_iota(jnp.int32, (2 * WINDOW, lanes), 1)
    ones_kv = jnp.ones((2 * WINDOW, lanes), BF16)
    kcol = lax.broadcasted_iota(jnp.int32, (1, 2 * WINDOW), 1)
    for sub in range(n_sub):
        r0 = sub * WINDOW
        blk = qi * n_sub + sub
        cur0 = pl.multiple_of(blk * WINDOW, WINDOW)
        prev0 = pl.multiple_of(jnp.maximum(blk - 1, 0) * WINDOW, WINDOW)
        for pair in range(SWA_KV_HEADS // 2):
            c0 = pair * 2 * HEAD_DIM
            k2 = jnp.concatenate([k_ref[pl.ds(prev0, WINDOW), c0:c0 + 2 * HEAD_DIM],
                                  k_ref[pl.ds(cur0, WINDOW), c0:c0 + 2 * HEAD_DIM]], axis=0)
            v2 = jnp.concatenate([v_ref[pl.ds(prev0, WINDOW), c0:c0 + 2 * HEAD_DIM],
                                  v_ref[pl.ds(cur0, WINDOW), c0:c0 + 2 * HEAD_DIM]], axis=0)
            zkv = jnp.zeros_like(v2)
            v_aug = [jnp.concatenate([jnp.where(lane_kv < HEAD_DIM, v2, zkv), ones_kv], axis=1),
                     jnp.concatenate([jnp.where(lane_kv >= HEAD_DIM, v2, zkv), ones_kv], axis=1)]
            rows = []
            for g in range(SWA_GROUP):
                t0 = (pair * SWA_GROUP + g) * 2 * HEAD_DIM
                qt = q_ref[r0:r0 + WINDOW, t0:t0 + 2 * HEAD_DIM]
                zq = jnp.zeros_like(qt)
                rows.append(jnp.where(lane_q < HEAD_DIM, qt, zq))
                rows.append(jnp.where(lane_q >= HEAD_DIM, qt, zq))
            q8 = jnp.concatenate(rows, axis=0)
            s = lax.dot_general(q8, k2, (((1,), (1,)), ((), ())), preferred_element_type=F32)
            for g in range(SWA_GROUP):
                o = None
                for e in range(2):
                    rb = (g * 2 + e) * WINDOW
                    sb = s[rb:rb + WINDOW] + tbl_ref[pair, rb:rb + WINDOW, :]
                    if sub == 0:
                        sb = sb + jnp.where((kcol < WINDOW) & (blk == 0), NEG_INF, 0.0)
                    sink = jnp.full((WINDOW, lanes), sink_ref[(2 * pair + e) * SWA_GROUP + g] * LOG2E, F32)
                    m = jnp.maximum(jnp.max(sb, axis=1, keepdims=True), sink)
                    p = jnp.exp2(sb - jnp.concatenate([m, m], axis=1))
                    pv = jnp.dot(p.astype(BF16), v_aug[e], preferred_element_type=F32)
                    part = pv[:, :lanes] / (pv[:, lanes:] + jnp.exp2(sink - m))
                    o = part if o is None else o + part
                t0 = (pair * SWA_GROUP + g) * 2 * HEAD_DIM
                o_ref[r0:r0 + WINDOW, t0:t0 + 2 * HEAD_DIM] = o.astype(o_ref.dtype)


def _swa_attention(proj, sinks, tbl, *, batch, seq, tq):
    nq = seq // tq
    q_blk = (3 * DIFF_WIDTH) // SWA_WIDTH
    kv_w = SWA_KV_HEADS * HEAD_DIM
    k_blk = (3 * DIFF_WIDTH + SWA_WIDTH) // kv_w
    return pl.pallas_call(
        functools.partial(_swa_kernel, tq=tq),
        out_shape=jax.ShapeDtypeStruct((batch * seq, SWA_WIDTH), BF16),
        grid=(batch, nq),
        in_specs=[
            pl.BlockSpec(memory_space=pltpu.SMEM),
            pl.BlockSpec(tbl.shape, lambda b, i: (0, 0, 0)),
            pl.BlockSpec((tq, SWA_WIDTH), lambda b, i: (b * nq + i, q_blk)),
            pl.BlockSpec((seq, kv_w), lambda b, i: (b, k_blk)),
            pl.BlockSpec((seq, kv_w), lambda b, i: (b, k_blk + 1)),
        ],
        out_specs=pl.BlockSpec((tq, SWA_WIDTH), lambda b, i: (b * nq + i, 0)),
        compiler_params=pltpu.CompilerParams(
            dimension_semantics=("parallel", "arbitrary"),
            vmem_limit_bytes=VMEM_LIMIT_BYTES,
        ),
        name="swa_attention",
    )(sinks, tbl, proj, proj, proj)


def _outproj_kernel(x_ref, oa_ref, ob_ref, wa_ref, wb_ref, nw_ref, x1_ref, h_ref, *, rc):
    for c in range(x_ref.shape[0] // rc):
        rows = slice(c * rc, (c + 1) * rc)
        y = jnp.dot(oa_ref[rows, :], wa_ref[...], preferred_element_type=F32)
        y = y + jnp.dot(ob_ref[rows, :], wb_ref[...], preferred_element_type=F32)
        x1 = x_ref[rows, :] + y
        x1_ref[rows, :] = x1
        h_ref[rows, :] = (_rms_scale(x1) * nw_ref[...]).astype(h_ref.dtype)


def _outproj(x2, oa, ob, w, norm_w, *, tm, rc):
    tokens, d = x2.shape
    resident = dict(pipeline_mode=pl.Buffered(1))
    return pl.pallas_call(
        functools.partial(_outproj_kernel, rc=rc),
        out_shape=(jax.ShapeDtypeStruct((tokens, d), F32), jax.ShapeDtypeStruct((tokens, d), BF16)),
        grid=(tokens // tm,),
        in_specs=[
            pl.BlockSpec((tm, d), lambda i: (i, 0)),
            pl.BlockSpec((tm, DIFF_WIDTH), lambda i: (i, 0)),
            pl.BlockSpec((tm, SWA_WIDTH), lambda i: (i, 0)),
            pl.BlockSpec((DIFF_WIDTH, d), lambda i: (0, 0), **resident),
            pl.BlockSpec((SWA_WIDTH, d), lambda i: (DIFF_WIDTH // SWA_WIDTH, 0), **resident),
            pl.BlockSpec((1, d), lambda i: (0, 0), **resident),
        ],
        out_specs=(pl.BlockSpec((tm, d), lambda i: (i, 0)), pl.BlockSpec((tm, d), lambda i: (i, 0))),
        compiler_params=pltpu.CompilerParams(
            dimension_semantics=("parallel",),
            vmem_limit_bytes=VMEM_LIMIT_BYTES,
        ),
        name="outproj_residual_norm",
    )(x2, oa, ob, w, w, norm_w)


def _ffn_kernel(h_ref, x1_ref, wg_ref, wu_ref, wd_ref, nw_ref, o_ref, *, rc):
    f = pl.program_id(1)

    @pl.when(f == 0)
    def _():
        o_ref[...] = x1_ref[...]

    def gated(c):
        h = h_ref[c * rc:(c + 1) * rc, :]
        g = jnp.dot(h, wg_ref[...], preferred_element_type=F32)
        u = jnp.dot(h, wu_ref[...], preferred_element_type=F32)
        return (g * (1.0 / (1.0 + jnp.exp(-g))) * u).astype(BF16)

    n_chunk = h_ref.shape[0] // rc
    a_next = gated(0)
    for c in range(n_chunk):
        a = a_next
        if c + 1 < n_chunk:
            a_next = gated(c + 1)
        o_ref[c * rc:(c + 1) * rc, :] += jnp.dot(a, wd_ref[...], preferred_element_type=F32)

    @pl.when(f == pl.num_programs(1) - 1)
    def _():
        o_ref[...] = _rms_scale(o_ref[...]) * nw_ref[...]


def _ffn(h2, x1, wg, wu, wd, norm_w, *, tm, tf, rc):
    tokens, d = x1.shape
    dff = wd.shape[0]
    return pl.pallas_call(
        functools.partial(_ffn_kernel, rc=rc),
        out_shape=jax.ShapeDtypeStruct((tokens, d), F32),
        grid=(tokens // tm, dff // tf),
        in_specs=[
            pl.BlockSpec((tm, d), lambda i, f: (i, 0)),
            pl.BlockSpec((tm, d), lambda i, f: (i, 0)),
            pl.BlockSpec((None, d, tf), lambda i, f: (f, 0, 0)),
            pl.BlockSpec((None, d, tf), lambda i, f: (f, 0, 0)),
            pl.BlockSpec((tf, d), lambda i, f: (f, 0)),
            pl.BlockSpec((1, d), lambda i, f: (0, 0)),
        ],
        out_specs=pl.BlockSpec((tm, d), lambda i, f: (i, 0)),
        compiler_params=pltpu.CompilerParams(
            dimension_semantics=("parallel", "arbitrary"),
            vmem_limit_bytes=VMEM_LIMIT_BYTES,
        ),
        name="swiglu_ffn_final_norm",
    )(h2, x1, wg, wu, wd, norm_w)


def kernel(x, attn_norm_w, w_in, lambda_q1, lambda_k1, lambda_q2, lambda_k2, subln_w, sinks, w_out,
           ffn_norm_w, w_gate, w_up, w_down, final_norm_w):
    batch, seq, d = x.shape
    depth = w_in.shape[0]
    assert (d, w_in.shape[2], w_gate.shape[2]) == (D_MODEL, IN_COLS, D_FF)
    tokens = batch * seq
    assert seq % KEY_BLOCK == 0 and seq % SWA_ROW_TILE == 0
    assert tokens % ROW_TILE == 0 and tokens % FFN_ROW_TILE == 0
    x2 = x.reshape(tokens, d)

    q_cols = np.ones((1, IN_COLS), np.float32)
    q_cols[:, :DIFF_WIDTH] = Q_SCALE
    q_cols[:, 3 * DIFF_WIDTH:3 * DIFF_WIDTH + SWA_WIDTH] = Q_SCALE
    col_scale = jnp.asarray(q_cols)
    diff_slopes2 = jnp.asarray((_alibi_slopes(DIFF_HEADS) * LOG2E).astype(np.float32))
    swa_tbl = jnp.asarray(_swa_tables())

    assert depth == 1
    l = 0
    lambda_init = 0.8 - 0.6 * math.exp(-0.3 * l)
    proj, wg_b, wu_b, wd_b, wo_b = _norm_inproj(
        x2, attn_norm_w[l].reshape(1, d), col_scale, w_in[l].astype(BF16), w_gate[l], w_up[l], w_down[l],
        w_out[l], tm=ROW_TILE, tn=COL_TILE, tf=COL_TILE)
    oa = _diff_attention(
        proj, diff_slopes2,
        lambda_q1[l].reshape(1, HEAD_DIM), lambda_k1[l].reshape(1, HEAD_DIM),
        lambda_q2[l].reshape(1, HEAD_DIM), lambda_k2[l].reshape(1, HEAD_DIM),
        subln_w[l].reshape(1, 2 * HEAD_DIM),
        batch=batch, seq=seq, tk=KEY_BLOCK, rc=ROW_CHUNK, lambda_init=lambda_init)
    ob = _swa_attention(proj, sinks[l], swa_tbl, batch=batch, seq=seq, tq=SWA_ROW_TILE)
    x1, h2 = _outproj(x2, oa, ob, wo_b, ffn_norm_w[l].reshape(1, d), tm=ROW_TILE, rc=ROW_CHUNK)
    out = _ffn(h2, x1, wg_b, wu_b, wd_b, final_norm_w.reshape(1, d), tm=FFN_ROW_TILE, tf=COL_TILE, rc=ROW_CHUNK)
    return out.reshape(batch, seq, d)
```

```python
import functools
import math

import jax
import jax.numpy as jnp
import numpy as np
from jax import lax
from jax.experimental import pallas as pl
from jax.experimental.pallas import tpu as pltpu

D_MODEL = 2048
HEAD_DIM = 64
DIFF_HEADS = 8
DIFF_WIDTH = 1024
SWA_Q_HEADS = 16
SWA_KV_HEADS = 4
SWA_GROUP = 4
SWA_WIDTH = 1024
WINDOW = 128
D_FF = 5632
IN_COLS = 4608
RMS_EPS = 1e-5
LOG2E = math.log2(math.e)
Q_SCALE = HEAD_DIM ** -0.5 * LOG2E
NEG_INF = float("-inf")

V7X_VMEM_BYTES = 64 * 1024 * 1024
VMEM_LIMIT_BYTES = V7X_VMEM_BYTES - 8 * 1024 * 1024

ROW_TILE = 512
SWA_ROW_TILE = 1024
FFN_ROW_TILE = 1024
COL_TILE = 512
ROW_CHUNK = 256
KEY_BLOCK = 512

F32 = jnp.float32
BF16 = jnp.bfloat16


def _alibi_slopes(n_heads):
    return np.array([2.0 ** (-8.0 * (h + 1) / n_heads) for h in range(n_heads)], dtype=np.float64)


def _rms_scale(x):
    return x * lax.rsqrt(jnp.mean(x * x, axis=-1, keepdims=True) + RMS_EPS)


def _swa_query_reorder(acc):
    pieces = [acc[:, (e * SWA_GROUP + g) * HEAD_DIM:(e * SWA_GROUP + g + 1) * HEAD_DIM]
              for g in range(SWA_GROUP) for e in range(2)]
    return jnp.concatenate(pieces, axis=1)


def _norm_inproj_kernel(x_ref, nw_ref, cs_ref, w_ref, wg_ref, wu_ref, wd_ref, wo_ref,
                        o_ref, wg_o, wu_o, wd_o, wo_o, h_ref, *, tn):
    h_ref[...] = (_rms_scale(x_ref[...]) * nw_ref[...]).astype(BF16)
    for j in range(w_ref.shape[1] // tn):
        cols = slice(j * tn, (j + 1) * tn)
        acc = jnp.dot(h_ref[...], w_ref[:, cols], preferred_element_type=F32)
        if 3 * DIFF_WIDTH <= j * tn < 3 * DIFF_WIDTH + SWA_WIDTH:
            acc = _swa_query_reorder(acc)
        o_ref[:, cols] = (acc * cs_ref[:, cols]).astype(o_ref.dtype)
    tf = wg_o.shape[2]
    for f in range(wg_o.shape[0]):
        wg_o[f] = wg_ref[:, f * tf:(f + 1) * tf].astype(BF16)
        wu_o[f] = wu_ref[:, f * tf:(f + 1) * tf].astype(BF16)
    wd_o[...] = wd_ref[...].astype(BF16)
    wo_o[...] = wo_ref[...].astype(BF16)


def _swa_out_row_block(i):
    n_diff = DIFF_WIDTH // HEAD_DIM
    t = i - n_diff
    per_pair = 2 * SWA_GROUP
    src = n_diff + (t // per_pair) * per_pair + (t % 2) * SWA_GROUP + (t % per_pair) // 2
    return jnp.where(i < n_diff, i, src)


def _norm_inproj(x2, norm_w, col_scale, w_in_b, w_gate, w_up, w_down, w_out, *, tm, tn, tf):
    tokens, d = x2.shape
    n = w_in_b.shape[1]
    dff = w_gate.shape[1]
    steps = tokens // tm
    assert w_out.shape[0] == steps * HEAD_DIM and d % steps == 0 and dff % steps == 0
    assert tn == 2 * SWA_GROUP * HEAD_DIM
    resident = dict(pipeline_mode=pl.Buffered(1))
    rows = lambda r, c: pl.BlockSpec((r, c), lambda i: (i, 0))
    col_blocks = lambda: pl.BlockSpec((dff // tf, d // steps, tf), lambda i: (0, i, 0))
    return pl.pallas_call(
        functools.partial(_norm_inproj_kernel, tn=tn),
        out_shape=(
            jax.ShapeDtypeStruct((tokens, n), BF16),
            jax.ShapeDtypeStruct((dff // tf, d, tf), BF16),
            jax.ShapeDtypeStruct((dff // tf, d, tf), BF16),
            jax.ShapeDtypeStruct(w_down.shape, BF16),
            jax.ShapeDtypeStruct(w_out.shape, BF16),
        ),
        grid=(steps,),
        in_specs=[
            pl.BlockSpec((tm, d), lambda i: (i, 0)),
            pl.BlockSpec((1, d), lambda i: (0, 0), **resident),
            pl.BlockSpec((1, n), lambda i: (0, 0), **resident),
            pl.BlockSpec((d, n), lambda i: (0, 0), **resident),
            rows(d // steps, dff),
            rows(d // steps, dff),
            rows(dff // steps, d),
            pl.BlockSpec((HEAD_DIM, d), lambda i: (_swa_out_row_block(i), 0)),
        ],
        out_specs=(
            pl.BlockSpec((tm, n), lambda i: (i, 0)),
            col_blocks(),
            col_blocks(),
            rows(dff // steps, d),
            rows(HEAD_DIM, d),
        ),
        scratch_shapes=[pltpu.VMEM((tm, d), BF16)],
        compiler_params=pltpu.CompilerParams(
            dimension_semantics=("parallel",),
            vmem_limit_bytes=VMEM_LIMIT_BYTES,
        ),
        name="norm_inproj",
    )(x2, norm_w, col_scale, w_in_b, w_gate, w_up, w_down, w_out)


def _diff_attn_kernel(slope_ref, lq1_ref, lk1_ref, lq2_ref, lk2_ref, q_ref, k_ref, v_ref, sw_ref,
                      o_ref, m_ref, acc_ref, *, tk, rc, lambda_init):
    seq = q_ref.shape[0]
    lanes = 2 * HEAD_DIM
    n_blk = seq // tk
    nt = (((1,), (1,)), ((), ()))
    slope2 = slope_ref[pl.program_id(1)]

    first_map = lax.broadcasted_iota(jnp.int32, (rc, lanes), 1) < HEAD_DIM
    kcol = lax.broadcasted_iota(jnp.int32, (1, tk), 1).astype(F32)
    ones = jnp.ones((tk, lanes), BF16)
    tri = lax.broadcasted_iota(jnp.int32, (rc, rc), 1) <= lax.broadcasted_iota(jnp.int32, (rc, rc), 0)
    lam = (jnp.exp(jnp.sum(lq1_ref[...] * lk1_ref[...], axis=-1, keepdims=True))
           - jnp.exp(jnp.sum(lq2_ref[...] * lk2_ref[...], axis=-1, keepdims=True))
           + lambda_init)

    steps = [(i, j, sub, mp) for j in range(n_blk) for i in range(j, n_blk)
             for sub in range(tk // rc) for mp in range(2)]
    steps = [st for st in steps if st[0] > 0] + [st for st in steps if st[0] == 0]
    map1_out = {}

    def n_cols(i, j, sub):
        return sub * rc + rc if i == j else tk

    def logits(i, j, sub, mp):
        k0, r0, ncol = j * tk, i * tk + sub * rc, n_cols(i, j, sub)
        qc = q_ref[r0:r0 + rc, :]
        qz = jnp.where(first_map if mp == 0 else ~first_map, qc, jnp.zeros_like(qc))
        bias = slope2 * (kcol[:, :ncol] + float(k0 - i * tk))
        s = lax.dot_general(qz, k_ref[k0:k0 + ncol, :], nt, preferred_element_type=F32) + bias
        if i == j:
            s_diag = jnp.where(tri, s[:, ncol - rc:], NEG_INF)
            s = jnp.concatenate([s[:, :ncol - rc], s_diag], axis=1) if ncol > rc else s_diag
        return s

    def update(i, j, sub, mp, s):
        k0, r0, ncol = j * tk, i * tk + sub * rc, n_cols(i, j, sub)
        vaug = jnp.concatenate([v_ref[k0:k0 + ncol, :], ones[:ncol]], axis=1)
        row0 = mp * seq + r0
        rows = slice(row0, row0 + rc)
        m_cur = jnp.max(s, axis=1, keepdims=True)
        if j == 0:
            m_new = jnp.broadcast_to(m_cur, (rc, lanes))
        else:
            m_prev = m_ref[rows, :]
            m_new = jnp.maximum(m_prev, m_cur)
        p = jnp.exp2(s - jnp.concatenate([m_new] * (ncol // lanes), axis=1))
        pv = jnp.dot(p.astype(BF16), vaug, preferred_element_type=F32)
        if j > 0:
            alpha = jnp.exp2(m_prev - m_new)
            pv = jnp.concatenate([alpha, alpha], axis=1) * acc_ref[rows, :] + pv
        if i > j:
            m_ref[rows, :] = m_new
            acc_ref[rows, :] = pv
        elif mp == 0:
            map1_out[r0] = pv[:, :lanes] / pv[:, lanes:]
        else:
            o = map1_out.pop(r0) - lam * (pv[:, :lanes] / pv[:, lanes:])
            o = _rms_scale(o) * sw_ref[...] * (1.0 - lambda_init)
            o_ref[r0:r0 + rc, :] = o.astype(o_ref.dtype)

    ahead = 2
    pending = [logits(*st) for st in steps[:ahead]]
    for n, step in enumerate(steps):
        s_cur = pending.pop(0)
        if n + ahead < len(steps):
            pending.append(logits(*steps[n + ahead]))
        update(*step, s_cur)


def _diff_attention(proj, slopes2, lq1, lk1, lq2, lk2, subln_w, *, batch, seq, tk, rc, lambda_init):
    lane_blk = 2 * HEAD_DIM
    k_blk0 = DIFF_WIDTH // lane_blk
    v_blk0 = 2 * DIFF_WIDTH // lane_blk
    vec = lambda: pl.BlockSpec((1, HEAD_DIM), lambda b, h: (0, 0))
    return pl.pallas_call(
        functools.partial(_diff_attn_kernel, tk=tk, rc=rc, lambda_init=lambda_init),
        out_shape=jax.ShapeDtypeStruct((batch * seq, DIFF_WIDTH), BF16),
        grid=(batch, DIFF_HEADS),
        in_specs=[
            pl.BlockSpec(memory_space=pltpu.SMEM),
            vec(), vec(), vec(), vec(),
            pl.BlockSpec((seq, lane_blk), lambda b, h: (b, h)),
            pl.BlockSpec((seq, lane_blk), lambda b, h: (b, k_blk0 + h)),
            pl.BlockSpec((seq, lane_blk), lambda b, h: (b, v_blk0 + h)),
            pl.BlockSpec((1, lane_blk), lambda b, h: (0, 0)),
        ],
        out_specs=pl.BlockSpec((seq, lane_blk), lambda b, h: (b, h)),
        scratch_shapes=[
            pltpu.VMEM((2 * seq, lane_blk), F32),
            pltpu.VMEM((2 * seq, 2 * lane_blk), F32),
        ],
        compiler_params=pltpu.CompilerParams(
            dimension_semantics=("parallel", "parallel"),
            vmem_limit_bytes=VMEM_LIMIT_BYTES,
        ),
        name="diff_attention",
    )(slopes2, lq1, lk1, lq2, lk2, proj, proj, proj, subln_w)


def _swa_tables():
    slopes = _alibi_slopes(SWA_Q_HEADS)
    i = np.arange(WINDOW)[:, None]
    j = np.arange(2 * WINDOW)[None, :]
    dist = WINDOW + i - j
    valid = (dist >= 0) & (dist < WINDOW)
    tbl = np.empty((SWA_KV_HEADS // 2, SWA_GROUP * 2 * WINDOW, 2 * WINDOW), np.float32)
    for pair in range(SWA_KV_HEADS // 2):
        for g in range(SWA_GROUP):
            for e in range(2):
                head = (2 * pair + e) * SWA_GROUP + g
                r0 = (g * 2 + e) * WINDOW
                tbl[pair, r0:r0 + WINDOW] = np.where(valid, -slopes[head] * LOG2E * dist, -np.inf)
    return tbl


def _swa_kernel(sink_ref, tbl_ref, q_ref, k_ref, v_ref, o_ref, *, tq):
    qi = pl.program_id(1)
    n_sub = tq // WINDOW
    lanes = 2 * HEAD_DIM
    lane_q = lax.broadcasted_iota(jnp.int32, (WINDOW, lanes), 1)
    lane_kv = lax.broadcasted_iota(jnp.int32, (2 * WINDOW, lanes), 1)
    ones_kv = jnp.ones((2 * WINDOW, lanes), BF16)
    kcol = lax.broadcasted_iota(jnp.int32, (1, 2 * WINDOW), 1)
    for sub in range(n_sub):
        r0 = sub * WINDOW
        blk = qi * n_sub + sub
        cur0 = pl.multiple_of(blk * WINDOW, WINDOW)
        prev0 = pl.multiple_of(jnp.maximum(blk - 1, 0) * WINDOW, WINDOW)
        for pair in range(SWA_KV_HEADS // 2):
            c0 = pair * 2 * HEAD_DIM
            k2 = jnp.concatenate([k_ref[pl.ds(prev0, WINDOW), c0:c0 + 2 * HEAD_DIM],
                                  k_ref[pl.ds(cur0, WINDOW), c0:c0 + 2 * HEAD_DIM]], axis=0)
            v2 = jnp.concatenate([v_ref[pl.ds(prev0, WINDOW), c0:c0 + 2 * HEAD_DIM],
                                  v_ref[pl.ds(cur0, WINDOW), c0:c0 + 2 * HEAD_DIM]], axis=0)
            zkv = jnp.zeros_like(v2)
            v_aug = [jnp.concatenate([jnp.where(lane_kv < HEAD_DIM, v2, zkv), ones_kv], axis=1),
                     jnp.concatenate([jnp.where(lane_kv >= HEAD_DIM, v2, zkv), ones_kv], axis=1)]
            rows = []
            for g in range(SWA_GROUP):
                t0 = (pair * SWA_GROUP + g) * 2 * HEAD_DIM
                qt = q_ref[r0:r0 + WINDOW, t0:t0 + 2 * HEAD_DIM]
                zq = jnp.zeros_like(qt)
                rows.append(jnp.where(lane_q < HEAD_DIM, qt, zq))
                rows.append(jnp.where(lane_q >= HEAD_DIM, qt, zq))
            q8 = jnp.concatenate(rows, axis=0)
            s = lax.dot_general(q8, k2, (((1,), (1,)), ((), ())), preferred_element_type=F32)
            for g in range(SWA_GROUP):
                o = None
                for e in range(2):
                    rb = (g * 2 + e) * WINDOW
                    sb = s[rb:rb + WINDOW] + tbl_ref[pair, rb:rb + WINDOW, :]
                    if sub == 0:
                        sb = sb + jnp.where((kcol < WINDOW) & (blk == 0), NEG_INF, 0.0)
                    sink = jnp.full((WINDOW, lanes), sink_ref[(2 * pair + e) * SWA_GROUP + g] * LOG2E, F32)
                    m = jnp.maximum(jnp.max(sb, axis=1, keepdims=True), sink)
                    p = jnp.exp2(sb - jnp.concatenate([m, m], axis=1))
                    pv = jnp.dot(p.astype(BF16), v_aug[e], preferred_element_type=F32)
                    part = pv[:, :lanes] / (pv[:, lanes:] + jnp.exp2(sink - m))
                    o = part if o is None else o + part
                t0 = (pair * SWA_GROUP + g) * 2 * HEAD_DIM
                o_ref[r0:r0 + WINDOW, t0:t0 + 2 * HEAD_DIM] = o.astype(o_ref.dtype)


def _swa_attention(proj, sinks, tbl, *, batch, seq, tq):
    nq = seq // tq
    q_blk = (3 * DIFF_WIDTH) // SWA_WIDTH
    kv_w = SWA_KV_HEADS * HEAD_DIM
    k_blk = (3 * DIFF_WIDTH + SWA_WIDTH) // kv_w
    return pl.pallas_call(
        functools.partial(_swa_kernel, tq=tq),
        out_shape=jax.ShapeDtypeStruct((batch * seq, SWA_WIDTH), BF16),
        grid=(batch, nq),
        in_specs=[
            pl.BlockSpec(memory_space=pltpu.SMEM),
            pl.BlockSpec(tbl.shape, lambda b, i: (0, 0, 0)),
            pl.BlockSpec((tq, SWA_WIDTH), lambda b, i: (b * nq + i, q_blk)),
            pl.BlockSpec((seq, kv_w), lambda b, i: (b, k_blk)),
            pl.BlockSpec((seq, kv_w), lambda b, i: (b, k_blk + 1)),
        ],
        out_specs=pl.BlockSpec((tq, SWA_WIDTH), lambda b, i: (b * nq + i, 0)),
        compiler_params=pltpu.CompilerParams(
            dimension_semantics=("parallel", "arbitrary"),
            vmem_limit_bytes=VMEM_LIMIT_BYTES,
        ),
        name="swa_attention",
    )(sinks, tbl, proj, proj, proj)


def _outproj_kernel(x_ref, oa_ref, ob_ref, wa_ref, wb_ref, nw_ref, x1_ref, h_ref, *, rc):
    for c in range(x_ref.shape[0] // rc):
        rows = slice(c * rc, (c + 1) * rc)
        y = jnp.dot(oa_ref[rows, :], wa_ref[...], preferred_element_type=F32)
        y = y + jnp.dot(ob_ref[rows, :], wb_ref[...], preferred_element_type=F32)
        x1 = x_ref[rows, :] + y
        x1_ref[rows, :] = x1
        h_ref[rows, :] = (_rms_scale(x1) * nw_ref[...]).astype(h_ref.dtype)


def _outproj(x2, oa, ob, w, norm_w, *, tm, rc):
    tokens, d = x2.shape
    resident = dict(pipeline_mode=pl.Buffered(1))
    return pl.pallas_call(
        functools.partial(_outproj_kernel, rc=rc),
        out_shape=(jax.ShapeDtypeStruct((tokens, d), F32), jax.ShapeDtypeStruct((tokens, d), BF16)),
        grid=(tokens // tm,),
        in_specs=[
            pl.BlockSpec((tm, d), lambda i: (i, 0)),
            pl.BlockSpec((tm, DIFF_WIDTH), lambda i: (i, 0)),
            pl.BlockSpec((tm, SWA_WIDTH), lambda i: (i, 0)),
            pl.BlockSpec((DIFF_WIDTH, d), lambda i: (0, 0), **resident),
            pl.BlockSpec((SWA_WIDTH, d), lambda i: (DIFF_WIDTH // SWA_WIDTH, 0), **resident),
            pl.BlockSpec((1, d), lambda i: (0, 0), **resident),
        ],
        out_specs=(pl.BlockSpec((tm, d), lambda i: (i, 0)), pl.BlockSpec((tm, d), lambda i: (i, 0))),
        compiler_params=pltpu.CompilerParams(
            dimension_semantics=("parallel",),
            vmem_limit_bytes=VMEM_LIMIT_BYTES,
        ),
        name="outproj_residual_norm",
    )(x2, oa, ob, w, w, norm_w)


def _ffn_kernel(h_ref, x1_ref, wg_ref, wu_ref, wd_ref, nw_ref, o_ref, *, rc):
    f = pl.program_id(1)

    @pl.when(f == 0)
    def _():
        o_ref[...] = x1_ref[...]

    def gated(c):
        h = h_ref[c * rc:(c + 1) * rc, :]
        g = jnp.dot(h, wg_ref[...], preferred_element_type=F32)
        u = jnp.dot(h, wu_ref[...], preferred_element_type=F32)
        return ((0.5 * g) * (1.0 + jnp.tanh(0.5 * g)) * u).astype(BF16)

    n_chunk = h_ref.shape[0] // rc
    a_next = gated(0)
    for c in range(n_chunk):
        a = a_next
        if c + 1 < n_chunk:
            a_next = gated(c + 1)
        o_ref[c * rc:(c + 1) * rc, :] += jnp.dot(a, wd_ref[...], preferred_element_type=F32)

    @pl.when(f == pl.num_programs(1) - 1)
    def _():
        o_ref[...] = _rms_scale(o_ref[...]) * nw_ref[...]


def _ffn(h2, x1, wg, wu, wd, norm_w, *, tm, tf, rc):
    tokens, d = x1.shape
    dff = wd.shape[0]
    return pl.pallas_call(
        functools.partial(_ffn_kernel, rc=rc),
        out_shape=jax.ShapeDtypeStruct((tokens, d), F32),
        grid=(tokens // tm, dff // tf),
        in_specs=[
            pl.BlockSpec((tm, d), lambda i, f: (i, 0)),
            pl.BlockSpec((tm, d), lambda i, f: (i, 0)),
            pl.BlockSpec((None, d, tf), lambda i, f: (f, 0, 0)),
            pl.BlockSpec((None, d, tf), lambda i, f: (f, 0, 0)),
            pl.BlockSpec((tf, d), lambda i, f: (f, 0)),
            pl.BlockSpec((1, d), lambda i, f: (0, 0)),
        ],
        out_specs=pl.BlockSpec((tm, d), lambda i, f: (i, 0)),
        compiler_params=pltpu.CompilerParams(
            dimension_semantics=("parallel", "arbitrary"),
            vmem_limit_bytes=VMEM_LIMIT_BYTES,
        ),
        name="swiglu_ffn_final_norm",
    )(h2, x1, wg, wu, wd, norm_w)


def kernel(x, attn_norm_w, w_in, lambda_q1, lambda_k1, lambda_q2, lambda_k2, subln_w, sinks, w_out,
           ffn_norm_w, w_gate, w_up, w_down, final_norm_w):
    batch, seq, d = x.shape
    depth = w_in.shape[0]
    assert (d, w_in.shape[2], w_gate.shape[2]) == (D_MODEL, IN_COLS, D_FF)
    tokens = batch * seq
    assert seq % KEY_BLOCK == 0 and seq % SWA_ROW_TILE == 0
    assert tokens % ROW_TILE == 0 and tokens % FFN_ROW_TILE == 0
    x2 = x.reshape(tokens, d)

    q_cols = np.ones((1, IN_COLS), np.float32)
    q_cols[:, :DIFF_WIDTH] = Q_SCALE
    q_cols[:, 3 * DIFF_WIDTH:3 * DIFF_WIDTH + SWA_WIDTH] = Q_SCALE
    col_scale = jnp.asarray(q_cols)
    diff_slopes2 = jnp.asarray((_alibi_slopes(DIFF_HEADS) * LOG2E).astype(np.float32))
    swa_tbl = jnp.asarray(_swa_tables())

    assert depth == 1
    l = 0
    lambda_init = 0.8 - 0.6 * math.exp(-0.3 * l)
    proj, wg_b, wu_b, wd_b, wo_b = _norm_inproj(
        x2, attn_norm_w[l].reshape(1, d), col_scale, w_in[l].astype(BF16), w_gate[l], w_up[l], w_down[l],
        w_out[l], tm=ROW_TILE, tn=COL_TILE, tf=COL_TILE)
    oa = _diff_attention(
        proj, diff_slopes2,
        lambda_q1[l].reshape(1, HEAD_DIM), lambda_k1[l].reshape(1, HEAD_DIM),
        lambda_q2[l].reshape(1, HEAD_DIM), lambda_k2[l].reshape(1, HEAD_DIM),
        subln_w[l].reshape(1, 2 * HEAD_DIM),
        batch=batch, seq=seq, tk=KEY_BLOCK, rc=ROW_CHUNK, lambda_init=lambda_init)
    ob = _swa_attention(proj, sinks[l], swa_tbl, batch=batch, seq=seq, tq=SWA_ROW_TILE)
    x1, h2 = _outproj(x2, oa, ob, wo_b, ffn_norm_w[l].reshape(1, d), tm=ROW_TILE, rc=ROW_CHUNK)
    out = _ffn(h2, x1, wg_b, wu_b, wd_b, final_norm_w.reshape(1, d), tm=FFN_ROW_TILE, tf=COL_TILE, rc=ROW_CHUNK)
    return out.reshape(batch, seq, d)
```
